```python
import math
import jax, jax.numpy as jnp
from jax import lax
import numpy as np

D_MODEL = 1024
BATCH = 8
SEQ = 2048
DEPTH = 4

N_HEADS = 16
HEAD_DIM = D_MODEL // N_HEADS
D_FF = 4 * D_MODEL
N_A_LAYERS = DEPTH // 2
N_B_LAYERS = DEPTH - N_A_LAYERS
MOBA_BLOCK = 256
MOBA_TOPK = 3
MOBA_Q_CHUNK = 16
FOX_Q_BLOCK = 128
RMS_EPS = 1e-6
NEG_INF = -1e30

kernel_name = "yoco_moba_fox_hybrid"


def rms_norm(x, g):
    xf = x.astype(jnp.float32)
    y = xf * lax.rsqrt(jnp.mean(xf * xf, axis=-1, keepdims=True) + RMS_EPS)
    return (y * g.astype(jnp.float32)).astype(x.dtype)


def alibi_slopes(n):
    return jnp.asarray(2.0 ** (-8.0 * np.arange(1, n + 1) / n), dtype=jnp.float32)


def split_heads(t):
    b, s, _ = t.shape
    return t.reshape(b, s, N_HEADS, HEAD_DIM).transpose(0, 2, 1, 3)


def merge_heads(t):
    b, h, s, d = t.shape
    return t.transpose(0, 2, 1, 3).reshape(b, s, h * d)


def squared_relu_mlp(x, w_up, w_down):
    return jnp.square(jax.nn.relu(x @ w_up)) @ w_down


def moba_attention(q, k, v):
    b, h, s, d = q.shape
    nb = -(-s // MOBA_BLOCK)
    pad = ((0, 0), (0, 0), (0, nb * MOBA_BLOCK - s), (0, 0))
    kb = jnp.pad(k, pad).reshape(b, h, nb, MOBA_BLOCK, d)
    vb = jnp.pad(v, pad).reshape(b, h, nb, MOBA_BLOCK, d)
    scale = 1.0 / math.sqrt(d)
    slopes = alibi_slopes(h)[None, :, None, None]
    n_sel = min(MOBA_TOPK, nb - 1)
    own_blk = jnp.arange(s) // MOBA_BLOCK
    bi = jnp.arange(b)[:, None, None, None]
    hi = jnp.arange(h)[None, :, None, None]

    if n_sel > 0:
        k_mean = jnp.mean(kb.astype(jnp.float32), axis=3)
        gate = jnp.einsum('bhsd,bhnd->bhsn', q.astype(jnp.float32), k_mean)
        past = jnp.arange(nb)[None, :] < own_blk[:, None]
        gate = jnp.where(past, gate, NEG_INF)
        _, sel = lax.top_k(gate, n_sel)
        sel_valid = jnp.arange(n_sel)[None, :] < own_blk[:, None]

    n_chunks = s // MOBA_Q_CHUNK
    blk_offs = jnp.arange(MOBA_BLOCK)

    def chunk(ci):
        t0 = ci * MOBA_Q_CHUNK
        qc = lax.dynamic_slice_in_dim(q, t0, MOBA_Q_CHUNK, axis=2)
        tq = t0 + jnp.arange(MOBA_Q_CHUNK)
        blk = t0 // MOBA_BLOCK
        k_own = lax.dynamic_index_in_dim(kb, blk, axis=2, keepdims=False)
        v_own = lax.dynamic_index_in_dim(vb, blk, axis=2, keepdims=False)
        ks_own = blk * MOBA_BLOCK + blk_offs
        dist_own = (tq[:, None] - ks_own[None, :]).astype(jnp.float32)
        s_own = (jnp.einsum('bhcd,bhkd->bhck', qc, k_own).astype(jnp.float32) * scale
                 - slopes * dist_own)
        s_own = jnp.where(dist_own >= 0, s_own, NEG_INF)
        if n_sel > 0:
            sel_c = lax.dynamic_slice_in_dim(sel, t0, MOBA_Q_CHUNK, axis=2)
            valid_c = lax.dynamic_slice_in_dim(sel_valid, t0, MOBA_Q_CHUNK, axis=0)
            k_sel = kb[bi, hi, sel_c]
            v_sel = vb[bi, hi, sel_c]
            ks_sel = sel_c[..., None] * MOBA_BLOCK + blk_offs
            dist_sel = (tq[:, None, None] - ks_sel).astype(jnp.float32)
            s_sel = (jnp.einsum('bhcd,bhcnkd->bhcnk', qc, k_sel).astype(jnp.float32) * scale
                     - slopes[..., None] * dist_sel)
            s_sel = jnp.where(valid_c[:, :, None], s_sel, NEG_INF)
            scores = jnp.concatenate(
                [s_sel.reshape(b, h, MOBA_Q_CHUNK, n_sel * MOBA_BLOCK), s_own], axis=-1)
            p = jax.nn.softmax(scores, axis=-1)
            p_sel = p[..., :n_sel * MOBA_BLOCK].reshape(b, h, MOBA_Q_CHUNK, n_sel, MOBA_BLOCK).astype(v.dtype)
            p_own = p[..., n_sel * MOBA_BLOCK:].astype(v.dtype)
            return (jnp.einsum('bhcnk,bhcnkd->bhcd', p_sel, v_sel)
                    + jnp.einsum('bhck,bhkd->bhcd', p_own, v_own))
        p_own = jax.nn.softmax(s_own, axis=-1).astype(v.dtype)
        return jnp.einsum('bhck,bhkd->bhcd', p_own, v_own)

    outs = lax.map(chunk, jnp.arange(n_chunks))
    return outs.transpose(1, 2, 0, 3, 4).reshape(b, h, s, d)


def shared_kv(h, kv_norm, w_kv, w_f, b_f):
    hn = rms_norm(h, kv_norm)
    k, v = jnp.split(hn @ w_kv, 2, axis=-1)
    log_f = jax.nn.log_sigmoid((hn @ w_f + b_f).astype(jnp.float32))
    c = jnp.cumsum(log_f, axis=1).transpose(0, 2, 1)
    return split_heads(k), split_heads(v), c


def forgetting_attention(q, k, v, c):
    b, h, s, d = q.shape
    scale = 1.0 / math.sqrt(d)
    kpos = jnp.arange(s)

    def block(qi):
        t0 = qi * FOX_Q_BLOCK
        qb = lax.dynamic_slice_in_dim(q, t0, FOX_Q_BLOCK, axis=2)
        cq = lax.dynamic_slice_in_dim(c, t0, FOX_Q_BLOCK, axis=2)
        tq = t0 + jnp.arange(FOX_Q_BLOCK)
        sc = (jnp.einsum('bhqd,bhkd->bhqk', qb, k).astype(jnp.float32) * scale
              + cq[..., None] - c[:, :, None, :])
        sc = jnp.where(kpos[None, :] <= tq[:, None], sc, NEG_INF)
        p = jax.nn.softmax(sc, axis=-1).astype(v.dtype)
        return jnp.einsum('bhqk,bhkd->bhqd', p, v)

    outs = lax.map(block, jnp.arange(s // FOX_Q_BLOCK))
    return outs.transpose(1, 2, 0, 3, 4).reshape(b, h, s, d)


def setup_inputs(seed: int = 0) -> dict:
    key = jax.random.key(seed)
    ks = jax.random.split(key, 14)
    D, F, H = D_MODEL, D_FF, N_HEADS
    nrm = lambda k, shape, fan_in, gain=1.0: (gain * fan_in ** -0.5) * jax.random.normal(k, shape, jnp.float32)
    return {
        "x": jax.random.normal(ks[0], (BATCH, SEQ, D), jnp.float32),
        "attn_norm": 1.0 + 0.02 * jax.random.normal(ks[1], (DEPTH, D), jnp.float32),
        "moba_w_qkv": nrm(ks[2], (N_A_LAYERS, D, 3 * D), D),
        "fox_w_q": nrm(ks[3], (N_B_LAYERS, D, D), D),
        "w_o": nrm(ks[4], (DEPTH, D, D), D, 0.5),
        "kv_norm": 1.0 + 0.02 * jax.random.normal(ks[5], (D,), jnp.float32),
        "w_kv": nrm(ks[6], (D, 2 * D), D),
        "w_f": nrm(ks[7], (D, H), D),
        "b_f": 3.0 + 0.5 * jax.random.normal(ks[8], (H,), jnp.float32),
        "mlp_norm": 1.0 + 0.02 * jax.random.normal(ks[9], (DEPTH, D), jnp.float32),
        "w_up": nrm(ks[10], (DEPTH, D, F), D),
        "w_down": nrm(ks[11], (DEPTH, F, D), F, 0.5),
        "final_norm": 1.0 + 0.02 * jax.random.normal(ks[12], (D,), jnp.float32),
    }


def reference(x, attn_norm, moba_w_qkv, fox_w_q, w_o, kv_norm, w_kv, w_f, b_f,
              mlp_norm, w_up, w_down, final_norm):
    h = x
    k_sh = v_sh = c_sh = None
    for layer in range(DEPTH):
        if layer == N_A_LAYERS:
            k_sh, v_sh, c_sh = shared_kv(h, kv_norm, w_kv, w_f, b_f)
        hn = rms_norm(h, attn_norm[layer])
        if layer < N_A_LAYERS:
            q, k, v = jnp.split(hn @ moba_w_qkv[layer], 3, axis=-1)
            mix = moba_attention(split_heads(q), split_heads(k), split_heads(v))
        else:
            q = split_heads(hn @ fox_w_q[layer - N_A_LAYERS])
            mix = forgetting_attention(q, k_sh, v_sh, c_sh)
        h = h + merge_heads(mix) @ w_o[layer]
        h = h + squared_relu_mlp(rms_norm(h, mlp_norm[layer]), w_up[layer], w_down[layer])
    return rms_norm(h, final_norm)
```

```python
import functools
import math

import numpy as np
import jax
import jax.numpy as jnp
from jax import lax
from jax.experimental import pallas as pl
from jax.experimental.pallas import tpu as pltpu

RMS_EPS = 1e-6
NEG_INF = -1e30
LOG2E = 1.4426950408889634
MOBA_BLOCK = 256
MOBA_TOPK = 3
HEAD_DIM = 64
PAIR = 2 * HEAD_DIM
V7X_VMEM_BYTES = 64 * 1024 * 1024
ROW_TILE = 512
FF_CHUNK = 1024

_NT = (((1,), (1,)), ((), ()))


def _vmem_limit(nbytes):
    return int(min(max(nbytes * 3 // 2, 16 * 1024 * 1024), V7X_VMEM_BYTES * 7 // 8))


def _rms(x, g):
    return x * lax.rsqrt(jnp.mean(x * x, axis=-1, keepdims=True) + RMS_EPS) * g


def _const_spec(shape):
    return pl.BlockSpec(shape, lambda *_: (0,) * len(shape), pipeline_mode=pl.Buffered(1))


def _store_vt(vt_ref, vt):
    for c in range(vt.shape[1] // MOBA_BLOCK):
        vt_ref[0, c] = vt[:, c * MOBA_BLOCK:(c + 1) * MOBA_BLOCK].astype(vt_ref.dtype)


def _qkv_kernel(h_ref, g_ref, wqk_ref, wvt_ref, qk_ref, vt_ref, *, q_scale):
    d = h_ref.shape[1]
    xn = _rms(h_ref[...], g_ref[...]).astype(jnp.bfloat16)
    q = jnp.dot(xn, wqk_ref[:, :d], preferred_element_type=jnp.float32)
    qk_ref[:, :d] = (q * q_scale).astype(qk_ref.dtype)
    k = jnp.dot(xn, wqk_ref[:, d:], preferred_element_type=jnp.float32)
    qk_ref[:, d:] = k.astype(qk_ref.dtype)
    _store_vt(vt_ref, lax.dot_general(wvt_ref[...], xn, _NT, preferred_element_type=jnp.float32))


def _qkv_proj(h, g, wqk, wvt, batch, seq, q_scale):
    n, d = h.shape
    tm = min(ROW_TILE, seq)
    nt = seq // tm
    est = 2 * tm * d * 4 + (wqk.size + wvt.size) * 2 + 2 * (tm * 2 * d + d * tm) * 2 + 4 * tm * d * 4
    return pl.pallas_call(
        functools.partial(_qkv_kernel, q_scale=q_scale),
        grid=(batch, nt),
        in_specs=[
            pl.BlockSpec((tm, d), lambda b, t: (b * nt + t, 0)),
            _const_spec((1, d)),
            _const_spec(wqk.shape),
            _const_spec(wvt.shape),
        ],
        out_specs=[
            pl.BlockSpec((tm, 2 * d), lambda b, t: (b * nt + t, 0)),
            pl.BlockSpec((1, tm // MOBA_BLOCK, d, MOBA_BLOCK), lambda b, t: (b, t, 0, 0)),
        ],
        out_shape=[
            jax.ShapeDtypeStruct((n, 2 * d), jnp.bfloat16),
            jax.ShapeDtypeStruct((batch, seq // MOBA_BLOCK, d, MOBA_BLOCK), jnp.bfloat16),
        ],
        compiler_params=pltpu.CompilerParams(
            dimension_semantics=("parallel", "parallel"), vmem_limit_bytes=_vmem_limit(est)),
        name="moba_qkv_proj",
    )(h, g, wqk, wvt)


def _q_kernel(h_ref, g_ref, wq_ref, q_ref, *, q_scale):
    xn = _rms(h_ref[...], g_ref[...]).astype(jnp.bfloat16)
    q = jnp.dot(xn, wq_ref[...], preferred_element_type=jnp.float32)
    q_ref[...] = (q * q_scale).astype(q_ref.dtype)


def _q_proj(h, g, wq, q_scale):
    n, d = h.shape
    tm = min(ROW_TILE, n)
    est = 2 * tm * d * 4 + wq.size * 2 + 2 * tm * d * 2 + 3 * tm * d * 4
    return pl.pallas_call(
        functools.partial(_q_kernel, q_scale=q_scale),
        grid=(n // tm,),
        in_specs=[pl.BlockSpec((tm, d), lambda r: (r, 0)), _const_spec((1, d)), _const_spec(wq.shape)],
        out_specs=pl.BlockSpec((tm, d), lambda r: (r, 0)),
        out_shape=jax.ShapeDtypeStruct((n, d), jnp.bfloat16),
        compiler_params=pltpu.CompilerParams(
            dimension_semantics=("parallel",), vmem_limit_bytes=_vmem_limit(est)),
        name="fox_q_proj",
    )(h, g, wq)


def _bf16_pieces(x):
    p1 = x.astype(jnp.bfloat16)
    r1 = x - p1.astype(jnp.float32)
    p2 = r1.astype(jnp.bfloat16)
    p3 = (r1 - p2.astype(jnp.float32)).astype(jnp.bfloat16)
    return p1, p2, p3


def _kv_kernel(h_ref, g_ref, wk_ref, wvt_ref, wf_ref, bf_ref, k_ref, vt_ref, nc_ref, carry_ref):
    tm = h_ref.shape[0]

    @pl.when(pl.program_id(1) == 0)
    def _():
        carry_ref[...] = jnp.zeros_like(carry_ref)

    xn = _rms(h_ref[...], g_ref[...]).astype(jnp.bfloat16)
    k_ref[...] = jnp.dot(xn, wk_ref[...], preferred_element_type=jnp.float32).astype(k_ref.dtype)
    _store_vt(vt_ref, lax.dot_general(wvt_ref[...], xn, _NT, preferred_element_type=jnp.float32))

    z = jnp.dot(xn, wf_ref[...], preferred_element_type=jnp.float32) + bf_ref[...]
    log_f = jnp.minimum(z, 0.0) - jnp.log1p(jnp.exp(-jnp.abs(z)))
    row = lax.broadcasted_iota(jnp.int32, (tm, tm), 0)
    col = lax.broadcasted_iota(jnp.int32, (tm, tm), 1)
    tri = jnp.where(col <= row, 1.0, 0.0).astype(jnp.bfloat16)
    c = carry_ref[...]
    for piece in _bf16_pieces(log_f):
        c = c + jnp.dot(tri, piece, preferred_element_type=jnp.float32)
    carry_ref[...] = c[tm - 1:tm, :]
    nc_ref[...] = c * (-LOG2E)


def _kv_proj(h, g, wk, wvt, wf, bf, batch, seq):
    n, d = h.shape
    nh = wf.shape[1]
    tm = min(ROW_TILE, seq)
    nt = seq // tm
    est = 2 * tm * d * 4 + (wk.size + wvt.size) * 2 + 4 * tm * d * 2 + 4 * tm * d * 4 + tm * tm * 8
    return pl.pallas_call(
        _kv_kernel,
        grid=(batch, nt),
        in_specs=[
            pl.BlockSpec((tm, d), lambda b, t: (b * nt + t, 0)),
            _const_spec((1, d)),
            _const_spec(wk.shape),
            _const_spec(wvt.shape),
            _const_spec(wf.shape),
            _const_spec((1, nh)),
        ],
        out_specs=[
            pl.BlockSpec((tm, d), lambda b, t: (b * nt + t, 0)),
            pl.BlockSpec((1, tm // MOBA_BLOCK, d, MOBA_BLOCK), lambda b, t: (b, t, 0, 0)),
            pl.BlockSpec((tm, nh), lambda b, t: (b * nt + t, 0)),
        ],
        out_shape=[
            jax.ShapeDtypeStruct((n, d), jnp.bfloat16),
            jax.ShapeDtypeStruct((batch, seq // MOBA_BLOCK, d, MOBA_BLOCK), jnp.bfloat16),
            jax.ShapeDtypeStruct((n, nh), jnp.float32),
        ],
        scratch_shapes=[pltpu.VMEM((1, nh), jnp.float32)],
        compiler_params=pltpu.CompilerParams(
            dimension_semantics=("parallel", "arbitrary"), vmem_limit_bytes=_vmem_limit(est)),
        name="shared_kv_proj",
    )(h, g, wk, wvt, wf, bf)


def _mlp_kernel(h_ref, mix_ref, wo_ref, g_ref, wup_ref, wdn_ref, gf_ref, out_ref, *, final_norm):
    ff = wup_ref.shape[1]
    fc = min(FF_CHUNK, ff)
    h1 = h_ref[...] + jnp.dot(mix_ref[...], wo_ref[...], preferred_element_type=jnp.float32)
    xn = _rms(h1, g_ref[...]).astype(jnp.bfloat16)
    acc = h1
    for c in range(ff // fc):
        u = jnp.dot(xn, wup_ref[:, c * fc:(c + 1) * fc], preferred_element_type=jnp.float32)
        a = jnp.square(jnp.maximum(u, 0.0)).astype(jnp.bfloat16)
        acc = acc + jnp.dot(a, wdn_ref[c * fc:(c + 1) * fc, :], preferred_element_type=jnp.float32)
    if final_norm:
        acc = _rms(acc, gf_ref[...])
    out_ref[...] = acc


def _attn_out_mlp(h, mix, wo, g, wup, wdn, gf, final_norm):
    n, d = h.shape
    ff = wup.shape[1]
    tm = min(ROW_TILE, n)
    fc = min(FF_CHUNK, ff)
    est = (4 * tm * d * 4 + 2 * tm * d * 2 + (wo.size + wup.size + wdn.size) * 2
           + 3 * tm * d * 4 + tm * fc * 6)
    return pl.pallas_call(
        functools.partial(_mlp_kernel, final_norm=final_norm),
        grid=(n // tm,),
        in_specs=[
            pl.BlockSpec((tm, d), lambda r: (r, 0)),
            pl.BlockSpec((tm, d), lambda r: (r, 0)),
            _const_spec(wo.shape),
            _const_spec((1, d)),
            _const_spec(wup.shape),
            _const_spec(wdn.shape),
            _const_spec((1, d)),
        ],
        out_specs=pl.BlockSpec((tm, d), lambda r: (r, 0)),
        out_shape=jax.ShapeDtypeStruct((n, d), jnp.float32),
        compiler_params=pltpu.CompilerParams(
            dimension_semantics=("parallel",), vmem_limit_bytes=_vmem_limit(est)),
        name="attn_out_mlp",
    )(h, mix, wo, g, wup, wdn, gf)


def _head_mask(q2, a):
    lane = lax.broadcasted_iota(jnp.int32, q2.shape, 1)
    keep = (lane < HEAD_DIM) if a == 0 else (lane >= HEAD_DIM)
    return jnp.where(keep, q2, jnp.zeros_like(q2))


def _finish_pair(out_ref, outs):
    ot = jnp.concatenate(outs, axis=0)
    out_ref[...] = ot.T.astype(out_ref.dtype)


def _moba_kernel(slope_ref, q_ref, k_ref, vt_ref, out_ref, kmean_ref, selb_ref):
    tq = q_ref.shape[0]
    nkb = vt_ref.shape[1]
    p = pl.program_id(1)
    j = pl.program_id(2)

    @pl.when(j == 0)
    def _():
        for i in range(nkb):
            kb = k_ref[i * MOBA_BLOCK:(i + 1) * MOBA_BLOCK, :].astype(jnp.float32)
            kmean_ref[i:i + 1, :] = jnp.sum(kb, axis=0, keepdims=True) * (1.0 / MOBA_BLOCK)

    q2 = q_ref[...]
    key_l = lax.broadcasted_iota(jnp.int32, (MOBA_BLOCK, tq), 0)
    qry_l = lax.broadcasted_iota(jnp.int32, (MOBA_BLOCK, tq), 1)
    rel = (key_l - qry_l).astype(jnp.float32)
    blk = lax.broadcasted_iota(jnp.int32, (nkb, tq), 0)
    kj = k_ref[pl.ds(pl.multiple_of(j * MOBA_BLOCK, MOBA_BLOCK), MOBA_BLOCK), :]

    outs = []
    for a in range(2):
        qa = _head_mask(q2, a)
        slope = slope_ref[pl.ds(2 * p + a, 1), :]
        alibi = slope * rel

        gate = lax.dot_general(kmean_ref[...], qa.astype(jnp.float32), _NT,
                               precision=lax.Precision.HIGHEST, preferred_element_type=jnp.float32)
        beaten = jnp.zeros((nkb, tq), jnp.float32)
        for ip in range(nkb):
            row = gate[ip:ip + 1, :]
            wins = jnp.where(row > gate, 1.0, jnp.where((row == gate) & (blk > ip), 1.0, 0.0))
            beaten = beaten + wins * jnp.where(ip < j, 1.0, 0.0)
        chosen = (blk < j) & (beaten < float(MOBA_TOPK))
        selb_ref[a] = jnp.where(chosen, 0.0, NEG_INF)

        s = lax.dot_general(kj, qa, _NT, preferred_element_type=jnp.float32) + alibi
        s = jnp.where(key_l <= qry_l, s, NEG_INF)
        m0 = jnp.max(s, axis=0, keepdims=True)
        pt = jnp.exp2(s - m0)
        l0 = jnp.sum(pt, axis=0, keepdims=True)
        acc0 = jnp.dot(vt_ref[0, j, a * HEAD_DIM:(a + 1) * HEAD_DIM, :], pt.astype(jnp.bfloat16),
                       preferred_element_type=jnp.float32)

        def past_block(i, carry, qa=qa, alibi=alibi, slope=slope, a=a):
            m, l, acc = carry
            ki = k_ref[pl.ds(pl.multiple_of(i * MOBA_BLOCK, MOBA_BLOCK), MOBA_BLOCK), :]
            s = lax.dot_general(ki, qa, _NT, preferred_element_type=jnp.float32) + alibi
            shift = selb_ref[a, pl.ds(i, 1), :] - slope * ((j - i) * MOBA_BLOCK).astype(jnp.float32)
            m_new = jnp.maximum(m, jnp.max(s, axis=0, keepdims=True) + shift)
            alpha = jnp.exp2(m - m_new)
            pt = jnp.exp2(s - (m_new - shift))
            l = alpha * l + jnp.sum(pt, axis=0, keepdims=True)
            pv = jnp.dot(vt_ref[0, i, a * HEAD_DIM:(a + 1) * HEAD_DIM, :], pt.astype(jnp.bfloat16),
                         preferred_element_type=jnp.float32)
            return m_new, l, alpha * acc + pv

        m, l, acc = lax.fori_loop(0, j, past_block, (m0, l0, acc0))
        outs.append(acc * (1.0 / l))

    _finish_pair(out_ref, outs)


def _moba_attention(qk, vt, slopes, batch, seq, d):
    nq = seq // MOBA_BLOCK
    npair = d // PAIR
    est = 2 * (MOBA_BLOCK * PAIR * 2 * 2 + seq * PAIR * 2 * 2) + slopes.size * 4 + 12 * MOBA_BLOCK * MOBA_BLOCK * 4
    return pl.pallas_call(
        _moba_kernel,
        grid=(batch, npair, nq),
        in_specs=[
            _const_spec(slopes.shape),
            pl.BlockSpec((MOBA_BLOCK, PAIR), lambda b, p, j: (b * nq + j, p)),
            pl.BlockSpec((seq, PAIR), lambda b, p, j: (b, npair + p)),
            pl.BlockSpec((1, nq, PAIR, MOBA_BLOCK), lambda b, p, j: (b, 0, p, 0)),
        ],
        out_specs=pl.BlockSpec((MOBA_BLOCK, PAIR), lambda b, p, j: (b * nq + j, p)),
        out_shape=jax.ShapeDtypeStruct((batch * seq, d), jnp.bfloat16),
        scratch_shapes=[
            pltpu.VMEM((nq, PAIR), jnp.float32),
            pltpu.VMEM((2, nq, MOBA_BLOCK), jnp.float32),
        ],
        compiler_params=pltpu.CompilerParams(
            dimension_semantics=("parallel", "parallel", "arbitrary"), vmem_limit_bytes=_vmem_limit(est)),
        name="moba_attention",
    )(slopes, qk, qk, vt)


def _fox_kernel(q_ref, k_ref, vt_ref, nc_ref, out_ref, cb_ref):
    tq = q_ref.shape[0]
    nkb = vt_ref.shape[1]
    p = pl.program_id(1)
    j = pl.program_id(2)

    @pl.when(j == 0)
    def _():
        head_l = lax.broadcasted_iota(jnp.int32, (MOBA_BLOCK, nc_ref.shape[1]), 1)
        for a in range(2):
            for i in range(nkb):
                cblk = nc_ref[i * MOBA_BLOCK:(i + 1) * MOBA_BLOCK, :]
                col = jnp.sum(jnp.where(head_l == 2 * p + a, cblk, 0.0), axis=1, keepdims=True)
                cb_ref[a, i] = jnp.broadcast_to(col, (MOBA_BLOCK, tq))

    q2 = q_ref[...]
    key_l = lax.broadcasted_iota(jnp.int32, (MOBA_BLOCK, tq), 0)
    qry_l = lax.broadcasted_iota(jnp.int32, (MOBA_BLOCK, tq), 1)
    kj = k_ref[pl.ds(pl.multiple_of(j * MOBA_BLOCK, MOBA_BLOCK), MOBA_BLOCK), :]

    outs = []
    for a in range(2):
        qa = _head_mask(q2, a)

        s = lax.dot_general(kj, qa, _NT, preferred_element_type=jnp.float32) + cb_ref[a, j]
        s = jnp.where(key_l <= qry_l, s, NEG_INF)
        m0 = jnp.max(s, axis=0, keepdims=True)
        pt = jnp.exp2(s - m0)
        l0 = jnp.sum(pt, axis=0, keepdims=True)
        acc0 = jnp.dot(vt_ref[0, j, a * HEAD_DIM:(a + 1) * HEAD_DIM, :], pt.astype(jnp.bfloat16),
                       preferred_element_type=jnp.float32)

        def past_block(i, carry, qa=qa, a=a):
            m, l, acc = carry
            ki = k_ref[pl.ds(pl.multiple_of(i * MOBA_BLOCK, MOBA_BLOCK), MOBA_BLOCK), :]
            s = lax.dot_general(ki, qa, _NT, preferred_element_type=jnp.float32) + cb_ref[a, i]
            m_new = jnp.maximum(m, jnp.max(s, axis=0, keepdims=True))
            alpha = jnp.exp2(m - m_new)
            pt = jnp.exp2(s - m_new)
            l = alpha * l + jnp.sum(pt, axis=0, keepdims=True)
            pv = jnp.dot(vt_ref[0, i, a * HEAD_DIM:(a + 1) * HEAD_DIM, :], pt.astype(jnp.bfloat16),
                         preferred_element_type=jnp.float32)
            return m_new, l, alpha * acc + pv

        m, l, acc = lax.fori_loop(0, j, past_block, (m0, l0, acc0))
        outs.append(acc * (1.0 / l))

    _finish_pair(out_ref, outs)


def _fox_attention(q, k, vt, nc, batch, seq, d):
    nq = seq // MOBA_BLOCK
    npair = d // PAIR
    nh = nc.shape[1]
    est = (2 * (MOBA_BLOCK * PAIR * 2 * 2 + seq * PAIR * 2 * 2 + seq * 128 * 4)
           + 2 * seq * MOBA_BLOCK * 4 + 12 * MOBA_BLOCK * MOBA_BLOCK * 4)
    return pl.pallas_call(
        _fox_kernel,
        grid=(batch, npair, nq),
        in_specs=[
            pl.BlockSpec((MOBA_BLOCK, PAIR), lambda b, p, j: (b * nq + j, p)),
            pl.BlockSpec((seq, PAIR), lambda b, p, j: (b, p)),
            pl.BlockSpec((1, nq, PAIR, MOBA_BLOCK), lambda b, p, j: (b, 0, p, 0)),
            pl.BlockSpec((seq, nh), lambda b, p, j: (b, 0)),
        ],
        out_specs=pl.BlockSpec((MOBA_BLOCK, PAIR), lambda b, p, j: (b * nq + j, p)),
        out_shape=jax.ShapeDtypeStruct((batch * seq, d), jnp.bfloat16),
        scratch_shapes=[pltpu.VMEM((2, nq, MOBA_BLOCK, MOBA_BLOCK), jnp.float32)],
        compiler_params=pltpu.CompilerParams(
            dimension_semantics=("parallel", "parallel", "arbitrary"), vmem_limit_bytes=_vmem_limit(est)),
        name="fox_attention",
    )(q, k, vt, nc)


def kernel(x, attn_norm, moba_w_qkv, fox_w_q, w_o, kv_norm, w_kv, w_f, b_f, mlp_norm, w_up, w_down, final_norm):
    batch, seq, d = x.shape
    depth = attn_norm.shape[0]
    n_a = moba_w_qkv.shape[0]
    nh = w_f.shape[1]
    assert d == nh * HEAD_DIM and d % PAIR == 0 and seq % MOBA_BLOCK == 0
    assert seq // MOBA_BLOCK - 1 >= MOBA_TOPK

    bf16 = jnp.bfloat16
    q_scale = LOG2E / math.sqrt(HEAD_DIM)
    slopes = 2.0 ** (-8.0 * np.arange(1, nh + 1) / nh) * LOG2E
    slopes = jnp.asarray(np.broadcast_to(slopes[:, None], (nh, MOBA_BLOCK)), jnp.float32)

    h = x.reshape(batch * seq, d)
    row = lambda v: v.reshape(1, -1)
    k_sh = vt_sh = nc_sh = None
    for layer in range(depth):
        if layer == n_a:
            k_sh, vt_sh, nc_sh = _kv_proj(
                h, row(kv_norm), w_kv[:, :d].astype(bf16), w_kv[:, d:].T.astype(bf16),
                w_f.astype(bf16), row(b_f), batch, seq)
        if layer < n_a:
            w = moba_w_qkv[layer]
            qk, vt = _qkv_proj(h, row(attn_norm[layer]), w[:, :2 * d].astype(bf16),
                               w[:, 2 * d:].T.astype(bf16), batch, seq, q_scale)
            mix = _moba_attention(qk, vt, slopes, batch, seq, d)
        else:
            q = _q_proj(h, row(attn_norm[layer]), fox_w_q[layer - n_a].astype(bf16), q_scale)
            mix = _fox_attention(q, k_sh, vt_sh, nc_sh, batch, seq, d)
        h = _attn_out_mlp(h, mix, w_o[layer].astype(bf16), row(mlp_norm[layer]),
                          w_up[layer].astype(bf16), w_down[layer].astype(bf16),
                          row(final_norm), final_norm=(layer == depth - 1))
    return h.reshape(batch, seq, d)
```

```python
import functools
import math

import numpy as np
import jax
import jax.numpy as jnp
from jax import lax
from jax.experimental import pallas as pl
from jax.experimental.pallas import tpu as pltpu

RMS_EPS = 1e-6
NEG_INF = -1e30
LOG2E = 1.4426950408889634
MOBA_BLOCK = 256
MOBA_TOPK = 3
HEAD_DIM = 64
PAIR = 2 * HEAD_DIM
V7X_VMEM_BYTES = 64 * 1024 * 1024
ROW_TILE = 512
FF_CHUNK = 1024
QK_ROWS = 256

_NT = (((1,), (1,)), ((), ()))


def _vmem_limit(nbytes):
    return int(min(max(nbytes * 3 // 2, 16 * 1024 * 1024), V7X_VMEM_BYTES * 7 // 8))


def _rms(x, g):
    return x * lax.rsqrt(jnp.mean(x * x, axis=-1, keepdims=True) + RMS_EPS) * g


def _const_spec(shape):
    return pl.BlockSpec(shape, lambda *_: (0,) * len(shape), pipeline_mode=pl.Buffered(1))


def _qkv_kernel(h_ref, g_ref, wqk_ref, wvt_ref, qk_ref, vt_ref, *, q_scale):
    d = h_ref.shape[1]
    xn = _rms(h_ref[...], g_ref[...]).astype(jnp.bfloat16)
    q = jnp.dot(xn, wqk_ref[:, :d], preferred_element_type=jnp.float32)
    qk_ref[:, :d] = (q * q_scale).astype(qk_ref.dtype)
    k = jnp.dot(xn, wqk_ref[:, d:], preferred_element_type=jnp.float32)
    qk_ref[:, d:] = k.astype(qk_ref.dtype)
    vt = lax.dot_general(wvt_ref[...], xn, _NT, preferred_element_type=jnp.float32)
    vt_ref[0] = vt.astype(vt_ref.dtype)


def _qkv_proj(h, g, wqk, wvt, batch, seq, q_scale):
    n, d = h.shape
    tm = min(ROW_TILE, seq)
    nt = seq // tm
    est = 2 * tm * d * 4 + (wqk.size + wvt.size) * 2 + 2 * (tm * 2 * d + d * tm) * 2 + 4 * tm * d * 4
    return pl.pallas_call(
        functools.partial(_qkv_kernel, q_scale=q_scale),
        grid=(batch, nt),
        in_specs=[
            pl.BlockSpec((tm, d), lambda b, t: (b * nt + t, 0)),
            _const_spec((1, d)),
            _const_spec(wqk.shape),
            _const_spec(wvt.shape),
        ],
        out_specs=[
            pl.BlockSpec((tm, 2 * d), lambda b, t: (b * nt + t, 0)),
            pl.BlockSpec((1, d, tm), lambda b, t: (b, 0, t)),
        ],
        out_shape=[
            jax.ShapeDtypeStruct((n, 2 * d), jnp.bfloat16),
            jax.ShapeDtypeStruct((batch, d, seq), jnp.bfloat16),
        ],
        compiler_params=pltpu.CompilerParams(
            dimension_semantics=("parallel", "parallel"), vmem_limit_bytes=_vmem_limit(est)),
        name="moba_qkv_proj",
    )(h, g, wqk, wvt)


def _q_kernel(h_ref, g_ref, wq_ref, q_ref, *, q_scale):
    xn = _rms(h_ref[...], g_ref[...]).astype(jnp.bfloat16)
    q = jnp.dot(xn, wq_ref[...], preferred_element_type=jnp.float32)
    q_ref[...] = (q * q_scale).astype(q_ref.dtype)


def _q_proj(h, g, wq, q_scale):
    n, d = h.shape
    tm = min(ROW_TILE, n)
    est = 2 * tm * d * 4 + wq.size * 2 + 2 * tm * d * 2 + 3 * tm * d * 4
    return pl.pallas_call(
        functools.partial(_q_kernel, q_scale=q_scale),
        grid=(n // tm,),
        in_specs=[pl.BlockSpec((tm, d), lambda r: (r, 0)), _const_spec((1, d)), _const_spec(wq.shape)],
        out_specs=pl.BlockSpec((tm, d), lambda r: (r, 0)),
        out_shape=jax.ShapeDtypeStruct((n, d), jnp.bfloat16),
        compiler_params=pltpu.CompilerParams(
            dimension_semantics=("parallel",), vmem_limit_bytes=_vmem_limit(est)),
        name="fox_q_proj",
    )(h, g, wq)


def _bf16_pieces(x):
    p1 = x.astype(jnp.bfloat16)
    r1 = x - p1.astype(jnp.float32)
    p2 = r1.astype(jnp.bfloat16)
    p3 = (r1 - p2.astype(jnp.float32)).astype(jnp.bfloat16)
    return p1, p2, p3


def _kv_kernel(h_ref, g_ref, wk_ref, wvt_ref, wf_ref, bf_ref, k_ref, vt_ref, nc_ref, carry_ref):
    tm = h_ref.shape[0]

    @pl.when(pl.program_id(1) == 0)
    def _():
        carry_ref[...] = jnp.zeros_like(carry_ref)

    xn = _rms(h_ref[...], g_ref[...]).astype(jnp.bfloat16)
    k_ref[...] = jnp.dot(xn, wk_ref[...], preferred_element_type=jnp.float32).astype(k_ref.dtype)
    vt = lax.dot_general(wvt_ref[...], xn, _NT, preferred_element_type=jnp.float32)
    vt_ref[0] = vt.astype(vt_ref.dtype)

    z = jnp.dot(xn, wf_ref[...], preferred_element_type=jnp.float32) + bf_ref[...]
    log_f = jnp.minimum(z, 0.0) - jnp.log1p(jnp.exp(-jnp.abs(z)))
    row = lax.broadcasted_iota(jnp.int32, (tm, tm), 0)
    col = lax.broadcasted_iota(jnp.int32, (tm, tm), 1)
    tri = jnp.where(col <= row, 1.0, 0.0).astype(jnp.bfloat16)
    c = carry_ref[...]
    for piece in _bf16_pieces(log_f):
        c = c + jnp.dot(tri, piece, preferred_element_type=jnp.float32)
    carry_ref[...] = c[tm - 1:tm, :]
    nc_ref[...] = c * (-LOG2E)


def _kv_proj(h, g, wk, wvt, wf, bf, batch, seq):
    n, d = h.shape
    nh = wf.shape[1]
    tm = min(ROW_TILE, seq)
    nt = seq // tm
    est = 2 * tm * d * 4 + (wk.size + wvt.size) * 2 + 4 * tm * d * 2 + 4 * tm * d * 4 + tm * tm * 8
    return pl.pallas_call(
        _kv_kernel,
        grid=(batch, nt),
        in_specs=[
            pl.BlockSpec((tm, d), lambda b, t: (b * nt + t, 0)),
            _const_spec((1, d)),
            _const_spec(wk.shape),
            _const_spec(wvt.shape),
            _const_spec(wf.shape),
            _const_spec((1, nh)),
        ],
        out_specs=[
            pl.BlockSpec((tm, d), lambda b, t: (b * nt + t, 0)),
            pl.BlockSpec((1, d, tm), lambda b, t: (b, 0, t)),
            pl.BlockSpec((tm, nh), lambda b, t: (b * nt + t, 0)),
        ],
        out_shape=[
            jax.ShapeDtypeStruct((n, d), jnp.bfloat16),
            jax.ShapeDtypeStruct((batch, d, seq), jnp.bfloat16),
            jax.ShapeDtypeStruct((n, nh), jnp.float32),
        ],
        scratch_shapes=[pltpu.VMEM((1, nh), jnp.float32)],
        compiler_params=pltpu.CompilerParams(
            dimension_semantics=("parallel", "arbitrary"), vmem_limit_bytes=_vmem_limit(est)),
        name="shared_kv_proj",
    )(h, g, wk, wvt, wf, bf)


def _mlp_kernel(h_ref, mix_ref, wo_ref, g_ref, wup_ref, wdn_ref, gf_ref, out_ref, *, final_norm):
    ff = wup_ref.shape[1]
    fc = min(FF_CHUNK, ff)
    h1 = h_ref[...] + jnp.dot(mix_ref[...], wo_ref[...], preferred_element_type=jnp.float32)
    xn = _rms(h1, g_ref[...]).astype(jnp.bfloat16)
    acc = h1
    for c in range(ff // fc):
        u = jnp.dot(xn, wup_ref[:, c * fc:(c + 1) * fc], preferred_element_type=jnp.float32)
        a = jnp.square(jnp.maximum(u, 0.0)).astype(jnp.bfloat16)
        acc = acc + jnp.dot(a, wdn_ref[c * fc:(c + 1) * fc, :], preferred_element_type=jnp.float32)
    if final_norm:
        acc = _rms(acc, gf_ref[...])
    out_ref[...] = acc


def _attn_out_mlp(h, mix, wo, g, wup, wdn, gf, final_norm):
    n, d = h.shape
    ff = wup.shape[1]
    tm = min(ROW_TILE, n)
    fc = min(FF_CHUNK, ff)
    est = (4 * tm * d * 4 + 2 * tm * d * 2 + (wo.size + wup.size + wdn.size) * 2
           + 3 * tm * d * 4 + tm * fc * 6)
    return pl.pallas_call(
        functools.partial(_mlp_kernel, final_norm=final_norm),
        grid=(n // tm,),
        in_specs=[
            pl.BlockSpec((tm, d), lambda r: (r, 0)),
            pl.BlockSpec((tm, d), lambda r: (r, 0)),
            _const_spec(wo.shape),
            _const_spec((1, d)),
            _const_spec(wup.shape),
            _const_spec(wdn.shape),
            _const_spec((1, d)),
        ],
        out_specs=pl.BlockSpec((tm, d), lambda r: (r, 0)),
        out_shape=jax.ShapeDtypeStruct((n, d), jnp.float32),
        compiler_params=pltpu.CompilerParams(
            dimension_semantics=("parallel",), vmem_limit_bytes=_vmem_limit(est)),
        name="attn_out_mlp",
    )(h, mix, wo, g, wup, wdn, gf)


def _head_mask(q2, a):
    lane = lax.broadcasted_iota(jnp.int32, q2.shape, 1)
    keep = (lane < HEAD_DIM) if a == 0 else (lane >= HEAD_DIM)
    return jnp.where(keep, q2, jnp.zeros_like(q2))


def _attend(qa, k_ref, vt_ref, a, nblk, x_ref, bias, shift):
    tq = qa.shape[0]
    nk = nblk * MOBA_BLOCK
    xs = x_ref.at[nblk % 2, a]
    key_l = lax.broadcasted_iota(jnp.int32, (QK_ROWS, tq), 0)
    qry_l = lax.broadcasted_iota(jnp.int32, (QK_ROWS, tq), 1)
    shifts = [shift(i) for i in range(nblk - 1)] + [None]
    m = None
    for i in range(nblk):
        bias_i = bias(i)
        for c in range(MOBA_BLOCK // QK_ROWS):
            r0 = i * MOBA_BLOCK + c * QK_ROWS
            s = lax.dot_general(k_ref[r0:r0 + QK_ROWS, :], qa, _NT, preferred_element_type=jnp.float32)
            x = s + bias_i[c * QK_ROWS:(c + 1) * QK_ROWS, :]
            if i == nblk - 1:
                x = jnp.where(key_l + c * QK_ROWS <= qry_l, x, NEG_INF)
            mi = jnp.max(x, axis=0, keepdims=True)
            if shifts[i] is not None:
                mi = mi + shifts[i]
            m = mi if m is None else jnp.maximum(m, mi)
            xs[r0:r0 + QK_ROWS, :] = x
    l = None
    o = None
    for i in range(nblk):
        mm = m if shifts[i] is None else m - shifts[i]
        pt = jnp.exp2(xs[i * MOBA_BLOCK:(i + 1) * MOBA_BLOCK, :] - mm)
        li = jnp.sum(pt, axis=0, keepdims=True)
        l = li if l is None else l + li
        oi = jnp.dot(vt_ref[0, a * HEAD_DIM:(a + 1) * HEAD_DIM, i * MOBA_BLOCK:(i + 1) * MOBA_BLOCK],
                     pt.astype(jnp.bfloat16), preferred_element_type=jnp.float32)
        o = oi if o is None else o + oi
    return o * (1.0 / l)


def _finish_pair(out_ref, jj, outs):
    ot = jnp.concatenate(outs, axis=0)
    out_ref[jj * MOBA_BLOCK:(jj + 1) * MOBA_BLOCK, :] = ot.T.astype(out_ref.dtype)


def _moba_qblock(jj, p, slope_ref, q_ref, k_ref, vt_ref, out_ref, kmean_ref, x_ref):
    tq = MOBA_BLOCK
    nkb = kmean_ref.shape[0]
    q2 = q_ref[jj * MOBA_BLOCK:(jj + 1) * MOBA_BLOCK, :]
    key_l = lax.broadcasted_iota(jnp.int32, (MOBA_BLOCK, tq), 0)
    qry_l = lax.broadcasted_iota(jnp.int32, (MOBA_BLOCK, tq), 1)
    rel = (key_l - qry_l).astype(jnp.float32)
    outs = []
    for a in range(2):
        qa = _head_mask(q2, a)
        slope = slope_ref[pl.ds(2 * p + a, 1), :]
        alibi = slope * rel

        if jj > MOBA_TOPK:
            gate = lax.dot_general(kmean_ref[...], qa.astype(jnp.float32), _NT,
                                   precision=lax.Precision.HIGHEST, preferred_element_type=jnp.float32)
            blk = lax.broadcasted_iota(jnp.int32, (nkb, tq), 0)
            beaten = jnp.zeros((nkb, tq), jnp.float32)
            for ip in range(jj):
                row = gate[ip:ip + 1, :]
                beaten = beaten + jnp.where(row > gate, 1.0, jnp.where((row == gate) & (blk > ip), 1.0, 0.0))
            unsel = jnp.where(beaten < float(MOBA_TOPK), 0.0, NEG_INF)
        else:
            unsel = None

        def shift(i, slope=slope, unsel=unsel):
            dist = slope * float((jj - i) * MOBA_BLOCK)
            return -dist if unsel is None else unsel[i:i + 1, :] - dist

        outs.append(_attend(qa, k_ref, vt_ref, a, jj + 1, x_ref, lambda i, alibi=alibi: alibi, shift))
    _finish_pair(out_ref, jj, outs)


def _moba_kernel(slope_ref, q_ref, k_ref, vt_ref, out_ref, kmean_ref, x_ref):
    nkb = kmean_ref.shape[0]
    p = pl.program_id(1)
    for i in range(nkb):
        kb = k_ref[i * MOBA_BLOCK:(i + 1) * MOBA_BLOCK, :].astype(jnp.float32)
        kmean_ref[i:i + 1, :] = jnp.sum(kb, axis=0, keepdims=True) * (1.0 / MOBA_BLOCK)
    for jj in range(nkb):
        _moba_qblock(jj, p, slope_ref, q_ref, k_ref, vt_ref, out_ref, kmean_ref, x_ref)


def _attn_vmem_estimate(seq):
    blocks = 2 * 4 * seq * PAIR * 2
    temps = 4 * seq * MOBA_BLOCK * (4 + 4 + 2)
    return blocks + temps


def _moba_attention(qk, vt, slopes, batch, seq, d):
    nq = seq // MOBA_BLOCK
    npair = d // PAIR
    est = _attn_vmem_estimate(seq) + slopes.size * 4
    return pl.pallas_call(
        _moba_kernel,
        grid=(batch, npair),
        in_specs=[
            _const_spec(slopes.shape),
            pl.BlockSpec((seq, PAIR), lambda b, p: (b, p)),
            pl.BlockSpec((seq, PAIR), lambda b, p: (b, npair + p)),
            pl.BlockSpec((1, PAIR, seq), lambda b, p: (b, p, 0)),
        ],
        out_specs=pl.BlockSpec((seq, PAIR), lambda b, p: (b, p)),
        out_shape=jax.ShapeDtypeStruct((batch * seq, d), jnp.bfloat16),
        scratch_shapes=[
            pltpu.VMEM((nq, PAIR), jnp.float32),
            pltpu.VMEM((2, 2, seq, MOBA_BLOCK), jnp.float32),
        ],
        compiler_params=pltpu.CompilerParams(
            dimension_semantics=("parallel", "parallel"), vmem_limit_bytes=_vmem_limit(est)),
        name="moba_attention",
    )(slopes, qk, qk, vt)


def _fox_qblock(jj, q_ref, k_ref, vt_ref, out_ref, cb_ref, x_ref):
    q2 = q_ref[jj * MOBA_BLOCK:(jj + 1) * MOBA_BLOCK, :]
    outs = []
    for a in range(2):
        bias = lambda i, a=a: cb_ref[a, i * MOBA_BLOCK:(i + 1) * MOBA_BLOCK, :]
        outs.append(_attend(_head_mask(q2, a), k_ref, vt_ref, a, jj + 1, x_ref, bias, lambda i: None))
    _finish_pair(out_ref, jj, outs)


def _fox_kernel(q_ref, k_ref, vt_ref, nc_ref, out_ref, cb_ref, x_ref):
    tq = MOBA_BLOCK
    nkb = k_ref.shape[0] // MOBA_BLOCK
    p = pl.program_id(1)
    head_l = lax.broadcasted_iota(jnp.int32, (MOBA_BLOCK, nc_ref.shape[1]), 1)
    for a in range(2):
        for i in range(nkb):
            cblk = nc_ref[i * MOBA_BLOCK:(i + 1) * MOBA_BLOCK, :]
            col = jnp.sum(jnp.where(head_l == 2 * p + a, cblk, 0.0), axis=1, keepdims=True)
            cb_ref[a, i * MOBA_BLOCK:(i + 1) * MOBA_BLOCK, :] = jnp.broadcast_to(col, (MOBA_BLOCK, tq))
    for jj in range(nkb):
        _fox_qblock(jj, q_ref, k_ref, vt_ref, out_ref, cb_ref, x_ref)


def _fox_attention(q, k, vt, nc, batch, seq, d):
    nq = seq // MOBA_BLOCK
    npair = d // PAIR
    nh = nc.shape[1]
    est = _attn_vmem_estimate(seq) + 2 * seq * 128 * 4 + 2 * seq * MOBA_BLOCK * 4
    return pl.pallas_call(
        _fox_kernel,
        grid=(batch, npair),
        in_specs=[
            pl.BlockSpec((seq, PAIR), lambda b, p: (b, p)),
            pl.BlockSpec((seq, PAIR), lambda b, p: (b, p)),
            pl.BlockSpec((1, PAIR, seq), lambda b, p: (b, p, 0)),
            pl.BlockSpec((seq, nh), lambda b, p: (b, 0)),
        ],
        out_specs=pl.BlockSpec((seq, PAIR), lambda b, p: (b, p)),
        out_shape=jax.ShapeDtypeStruct((batch * seq, d), jnp.bfloat16),
        scratch_shapes=[
            pltpu.VMEM((2, seq, MOBA_BLOCK), jnp.float32),
            pltpu.VMEM((2, 2, seq, MOBA_BLOCK), jnp.float32),
        ],
        compiler_params=pltpu.CompilerParams(
            dimension_semantics=("parallel", "parallel"), vmem_limit_bytes=_vmem_limit(est)),
        name="fox_attention",
    )(q, k, vt, nc)


def kernel(x, attn_norm, moba_w_qkv, fox_w_q, w_o, kv_norm, w_kv, w_f, b_f, mlp_norm, w_up, w_down, final_norm):
    batch, seq, d = x.shape
    depth = attn_norm.shape[0]
    n_a = moba_w_qkv.shape[0]
    nh = w_f.shape[1]
    assert d == nh * HEAD_DIM and d % PAIR == 0 and seq % MOBA_BLOCK == 0
    assert seq // MOBA_BLOCK - 1 >= MOBA_TOPK

    bf16 = jnp.bfloat16
    q_scale = LOG2E / math.sqrt(HEAD_DIM)
    slopes = 2.0 ** (-8.0 * np.arange(1, nh + 1) / nh) * LOG2E
    slopes = jnp.asarray(np.broadcast_to(slopes[:, None], (nh, MOBA_BLOCK)), jnp.float32)

    h = x.reshape(batch * seq, d)
    row = lambda v: v.reshape(1, -1)
    k_sh = vt_sh = nc_sh = None
    for layer in range(depth):
        if layer == n_a:
            k_sh, vt_sh, nc_sh = _kv_proj(
                h, row(kv_norm), w_kv[:, :d].astype(bf16), w_kv[:, d:].T.astype(bf16),
                w_f.astype(bf16), row(b_f), batch, seq)
        if layer < n_a:
            w = moba_w_qkv[layer]
            qk, vt = _qkv_proj(h, row(attn_norm[layer]), w[:, :2 * d].astype(bf16),
                               w[:, 2 * d:].T.astype(bf16), batch, seq, q_scale)
            mix = _moba_attention(qk, vt, slopes, batch, seq, d)
        else:
            q = _q_proj(h, row(attn_norm[layer]), fox_w_q[layer - n_a].astype(bf16), q_scale)
            mix = _fox_attention(q, k_sh, vt_sh, nc_sh, batch, seq, d)
        h = _attn_out_mlp(h, mix, w_o[layer].astype(bf16), row(mlp_norm[layer]),
                          w_up[layer].astype(bf16), w_down[layer].astype(bf16),
                          row(final_norm), final_norm=(layer == depth - 1))
    return h.reshape(batch, seq, d)
```

```python
import functools
import math

import numpy as np
import jax
import jax.numpy as jnp
from jax import lax
from jax.experimental import pallas as pl
from jax.experimental.pallas import tpu as pltpu

RMS_EPS = 1e-6
NEG_INF = -1e30
LOG2E = 1.4426950408889634
MOBA_BLOCK = 256
MOBA_TOPK = 3
HEAD_DIM = 64
PAIR = 2 * HEAD_DIM
V7X_VMEM_BYTES = 64 * 1024 * 1024
ROW_TILE = 512
FF_CHUNK = 1024
N_PIECES = 3
ONES_ROWS = 16
SCORE_SLOTS = 4
LOOKAHEAD = SCORE_SLOTS - 1

_NT = (((1,), (1,)), ((), ()))


def _vmem_limit(nbytes):
    return int(min(max(nbytes * 3 // 2, 16 * 1024 * 1024), V7X_VMEM_BYTES * 7 // 8))


def _rms(x, g):
    return x * lax.rsqrt(jnp.mean(x * x, axis=-1, keepdims=True) + RMS_EPS) * g


def _const_spec(shape):
    return pl.BlockSpec(shape, lambda *_: (0,) * len(shape), pipeline_mode=pl.Buffered(1))


def _qkv_kernel(h_ref, g_ref, wqk_ref, wvt_ref, qk_ref, vt_ref, *, q_scale):
    d = h_ref.shape[1]
    xn = _rms(h_ref[...], g_ref[...]).astype(jnp.bfloat16)
    q = jnp.dot(xn, wqk_ref[:, :d], preferred_element_type=jnp.float32)
    qk_ref[:, :d] = (q * q_scale).astype(qk_ref.dtype)
    k = jnp.dot(xn, wqk_ref[:, d:], preferred_element_type=jnp.float32)
    qk_ref[:, d:] = k.astype(qk_ref.dtype)
    vt = lax.dot_general(wvt_ref[...], xn, _NT, preferred_element_type=jnp.float32)
    vt_ref[0] = vt.astype(vt_ref.dtype)


def _qkv_proj(h, g, wqk, wvt, batch, seq, q_scale):
    n, d = h.shape
    tm = min(ROW_TILE, seq)
    nt = seq // tm
    est = 2 * tm * d * 4 + (wqk.size + wvt.size) * 2 + 2 * (tm * 2 * d + d * tm) * 2 + 4 * tm * d * 4
    return pl.pallas_call(
        functools.partial(_qkv_kernel, q_scale=q_scale),
        grid=(batch, nt),
        in_specs=[
            pl.BlockSpec((tm, d), lambda b, t: (b * nt + t, 0)),
            _const_spec((1, d)),
            _const_spec(wqk.shape),
            _const_spec(wvt.shape),
        ],
        out_specs=[
            pl.BlockSpec((tm, 2 * d), lambda b, t: (b * nt + t, 0)),
            pl.BlockSpec((1, d, tm), lambda b, t: (b, 0, t)),
        ],
        out_shape=[
            jax.ShapeDtypeStruct((n, 2 * d), jnp.bfloat16),
            jax.ShapeDtypeStruct((batch, d, seq), jnp.bfloat16),
        ],
        compiler_params=pltpu.CompilerParams(
            dimension_semantics=("parallel", "parallel"), vmem_limit_bytes=_vmem_limit(est)),
        name="moba_qkv_proj",
    )(h, g, wqk, wvt)


def _q_kernel(h_ref, g_ref, wq_ref, q_ref, *, q_scale):
    xn = _rms(h_ref[...], g_ref[...]).astype(jnp.bfloat16)
    q = jnp.dot(xn, wq_ref[...], preferred_element_type=jnp.float32)
    q_ref[...] = (q * q_scale).astype(q_ref.dtype)


def _q_proj(h, g, wq, q_scale):
    n, d = h.shape
    tm = min(ROW_TILE, n)
    est = 2 * tm * d * 4 + wq.size * 2 + 2 * tm * d * 2 + 3 * tm * d * 4
    return pl.pallas_call(
        functools.partial(_q_kernel, q_scale=q_scale),
        grid=(n // tm,),
        in_specs=[pl.BlockSpec((tm, d), lambda r: (r, 0)), _const_spec((1, d)), _const_spec(wq.shape)],
        out_specs=pl.BlockSpec((tm, d), lambda r: (r, 0)),
        out_shape=jax.ShapeDtypeStruct((n, d), jnp.bfloat16),
        compiler_params=pltpu.CompilerParams(
            dimension_semantics=("parallel",), vmem_limit_bytes=_vmem_limit(est)),
        name="fox_q_proj",
    )(h, g, wq)


def _bf16_pieces(x):
    p1 = x.astype(jnp.bfloat16)
    r1 = x - p1.astype(jnp.float32)
    p2 = r1.astype(jnp.bfloat16)
    p3 = (r1 - p2.astype(jnp.float32)).astype(jnp.bfloat16)
    return p1, p2, p3


def _kv_kernel(h_ref, g_ref, wk_ref, wvt_ref, wf_ref, bf_ref, place_ref, k_ref, vt_ref, kf_ref, carry_ref):
    tm = h_ref.shape[0]
    nh = wf_ref.shape[1] // N_PIECES

    @pl.when(pl.program_id(1) == 0)
    def _():
        carry_ref[...] = jnp.zeros_like(carry_ref)

    xn = _rms(h_ref[...], g_ref[...]).astype(jnp.bfloat16)
    k_ref[...] = jnp.dot(xn, wk_ref[...], preferred_element_type=jnp.float32).astype(k_ref.dtype)
    vt = lax.dot_general(wvt_ref[...], xn, _NT, preferred_element_type=jnp.float32)
    vt_ref[0] = vt.astype(vt_ref.dtype)

    z = jnp.dot(xn, wf_ref[...], preferred_element_type=jnp.float32) + bf_ref[...]
    log_f = jnp.minimum(z, 0.0) - jnp.log1p(jnp.exp(-jnp.abs(z)))
    row = lax.broadcasted_iota(jnp.int32, (tm, tm), 0)
    col = lax.broadcasted_iota(jnp.int32, (tm, tm), 1)
    tri = jnp.where(col <= row, 1.0, 0.0).astype(jnp.bfloat16)
    c = carry_ref[...]
    for piece in _bf16_pieces(log_f):
        c = c + jnp.dot(tri, piece, preferred_element_type=jnp.float32)
    carry_ref[...] = c[tm - 1:tm, :]
    p1, p2, p3 = _bf16_pieces(c * (-LOG2E))
    col_f = lax.broadcasted_iota(jnp.int32, c.shape, 1)
    pcs = jnp.where(col_f < nh, p1, jnp.where(col_f < 2 * nh, p2, p3))
    kf = jnp.dot(pcs, place_ref[...], preferred_element_type=jnp.float32)
    kf_ref[...] = kf.astype(kf_ref.dtype)


def _kv_proj(h, g, wk, wvt, wf, bf, place, batch, seq):
    n, d = h.shape
    nf = wf.shape[1]
    tm = min(ROW_TILE, seq)
    nt = seq // tm
    est = 2 * tm * d * 4 + (wk.size + wvt.size) * 2 + 6 * tm * d * 2 + 5 * tm * d * 4 + tm * tm * 8
    return pl.pallas_call(
        _kv_kernel,
        grid=(batch, nt),
        in_specs=[
            pl.BlockSpec((tm, d), lambda b, t: (b * nt + t, 0)),
            _const_spec((1, d)),
            _const_spec(wk.shape),
            _const_spec(wvt.shape),
            _const_spec(wf.shape),
            _const_spec((1, nf)),
            _const_spec(place.shape),
        ],
        out_specs=[
            pl.BlockSpec((tm, d), lambda b, t: (b * nt + t, 0)),
            pl.BlockSpec((1, d, tm), lambda b, t: (b, 0, t)),
            pl.BlockSpec((tm, d), lambda b, t: (b * nt + t, 0)),
        ],
        out_shape=[
            jax.ShapeDtypeStruct((n, d), jnp.bfloat16),
            jax.ShapeDtypeStruct((batch, d, seq), jnp.bfloat16),
            jax.ShapeDtypeStruct((n, d), jnp.bfloat16),
        ],
        scratch_shapes=[pltpu.VMEM((1, nf), jnp.float32)],
        compiler_params=pltpu.CompilerParams(
            dimension_semantics=("parallel", "arbitrary"), vmem_limit_bytes=_vmem_limit(est)),
        name="shared_kv_proj",
    )(h, g, wk, wvt, wf, bf, place)


def _mlp_kernel(h_ref, mix_ref, wo_ref, g_ref, wup_ref, wdn_ref, gf_ref, out_ref, *, final_norm):
    ff = wup_ref.shape[1]
    fc = min(FF_CHUNK, ff)
    h1 = h_ref[...] + jnp.dot(mix_ref[...], wo_ref[...], preferred_element_type=jnp.float32)
    xn = _rms(h1, g_ref[...]).astype(jnp.bfloat16)
    acc = h1
    for c in range(ff // fc):
        u = jnp.dot(xn, wup_ref[:, c * fc:(c + 1) * fc], preferred_element_type=jnp.float32)
        a = jnp.square(jnp.maximum(u, 0.0)).astype(jnp.bfloat16)
        acc = acc + jnp.dot(a, wdn_ref[c * fc:(c + 1) * fc, :], preferred_element_type=jnp.float32)
    if final_norm:
        acc = _rms(acc, gf_ref[...])
    out_ref[...] = acc


def _attn_out_mlp(h, mix, wo, g, wup, wdn, gf, final_norm):
    n, d = h.shape
    ff = wup.shape[1]
    tm = min(ROW_TILE, n)
    fc = min(FF_CHUNK, ff)
    est = (4 * tm * d * 4 + 2 * tm * d * 2 + (wo.size + wup.size + wdn.size) * 2
           + 3 * tm * d * 4 + tm * fc * 6)
    return pl.pallas_call(
        functools.partial(_mlp_kernel, final_norm=final_norm),
        grid=(n // tm,),
        in_specs=[
            pl.BlockSpec((tm, d), lambda r: (r, 0)),
            pl.BlockSpec((tm, d), lambda r: (r, 0)),
            _const_spec(wo.shape),
            _const_spec((1, d)),
            _const_spec(wup.shape),
            _const_spec(wdn.shape),
            _const_spec((1, d)),
        ],
        out_specs=pl.BlockSpec((tm, d), lambda r: (r, 0)),
        out_shape=jax.ShapeDtypeStruct((n, d), jnp.float32),
        compiler_params=pltpu.CompilerParams(
            dimension_semantics=("parallel",), vmem_limit_bytes=_vmem_limit(est)),
        name="attn_out_mlp",
    )(h, mix, wo, g, wup, wdn, gf)


def _head_mask(q2, a):
    lane = lax.broadcasted_iota(jnp.int32, q2.shape, 1)
    keep = (lane < HEAD_DIM) if a == 0 else (lane >= HEAD_DIM)
    return jnp.where(keep, q2, jnp.zeros_like(q2))


class _Unit:
    def __init__(self, jj, a, q_ext, shift):
        self.jj, self.a, self.q_ext = jj, a, q_ext
        self.nblk = jj + 1
        self.shifts = [shift(i) for i in range(jj)] + [None]
        self.m = None
        self.o = None


def _score_tile(u, i, k_ref, kf_ref, x_ref):
    tq = u.q_ext.shape[0]
    rows = slice(i * MOBA_BLOCK, (i + 1) * MOBA_BLOCK)
    k_ext = jnp.concatenate([k_ref[rows, :], kf_ref[rows, :]], axis=1)
    x = lax.dot_general(k_ext, u.q_ext, _NT, preferred_element_type=jnp.float32)
    if i == u.nblk - 1:
        key_l = lax.broadcasted_iota(jnp.int32, (MOBA_BLOCK, tq), 0)
        qry_l = lax.broadcasted_iota(jnp.int32, (MOBA_BLOCK, tq), 1)
        x = jnp.where(key_l <= qry_l, x, NEG_INF)
    mi = jnp.max(x, axis=0, keepdims=True)
    if u.shifts[i] is not None:
        mi = mi + u.shifts[i]
    u.m = mi if u.m is None else jnp.maximum(u.m, mi)
    x_ref[u.jj % 2, u.a, rows, :] = x


def _value_tile(u, i, vt_ref, x_ref):
    rows = slice(i * MOBA_BLOCK, (i + 1) * MOBA_BLOCK)
    mm = u.m if u.shifts[i] is None else u.m - u.shifts[i]
    pt = jnp.exp2(x_ref[u.jj % 2, u.a, rows, :] - mm).astype(jnp.bfloat16)
    ones = jnp.ones((ONES_ROWS, MOBA_BLOCK), jnp.bfloat16)
    v_ext = jnp.concatenate([vt_ref[0, u.a * HEAD_DIM:(u.a + 1) * HEAD_DIM, rows], ones], axis=0)
    oi = jnp.dot(v_ext, pt, preferred_element_type=jnp.float32)
    u.o = oi if u.o is None else u.o + oi


def _sweep(units, k_ref, kf_ref, vt_ref, out_ref, x_ref):
    units = list(units)
    for u in units[:LOOKAHEAD]:
        for i in range(u.nblk):
            _score_tile(u, i, k_ref, kf_ref, x_ref)
    done = {}
    for n, u in enumerate(units):
        nxt = units[n + LOOKAHEAD] if n + LOOKAHEAD < len(units) else None
        for i in range(max(u.nblk, nxt.nblk if nxt else 0)):
            if nxt is not None and i < nxt.nblk:
                _score_tile(nxt, i, k_ref, kf_ref, x_ref)
            if i < u.nblk:
                _value_tile(u, i, vt_ref, x_ref)
        done.setdefault(u.jj, []).append(u.o[:HEAD_DIM, :] * (1.0 / u.o[HEAD_DIM:HEAD_DIM + 1, :]))
        if len(done[u.jj]) == 2:
            ot = jnp.concatenate(done.pop(u.jj), axis=0)
            out_ref[u.jj * MOBA_BLOCK:(u.jj + 1) * MOBA_BLOCK, :] = ot.T.astype(out_ref.dtype)


def _moba_units(p, qf_ref, q_ref, kmean_ref):
    tq = MOBA_BLOCK
    nkb = kmean_ref.shape[0]
    for jj in range(nkb):
        q2 = q_ref[jj * MOBA_BLOCK:(jj + 1) * MOBA_BLOCK, :]
        for a in range(2):
            qa = _head_mask(q2, a)
            qfeat = jnp.broadcast_to(qf_ref[pl.ds(2 * p + a, 1), :], (tq, PAIR)).astype(jnp.bfloat16)
            if jj > MOBA_TOPK:
                gate = lax.dot_general(kmean_ref[...], qa.astype(jnp.float32), _NT,
                                       precision=lax.Precision.HIGHEST, preferred_element_type=jnp.float32)
                blk = lax.broadcasted_iota(jnp.int32, (nkb, tq), 0)
                beaten = jnp.zeros((nkb, tq), jnp.float32)
                for ip in range(jj):
                    row = gate[ip:ip + 1, :]
                    beaten = beaten + jnp.where(row > gate, 1.0, jnp.where((row == gate) & (blk > ip), 1.0, 0.0))
                unsel = jnp.where(beaten < float(MOBA_TOPK), 0.0, NEG_INF)
                shift = lambda i, unsel=unsel: unsel[i:i + 1, :]
            else:
                shift = lambda i: None
            yield _Unit(jj, a, jnp.concatenate([qa, qfeat], axis=1), shift)


def _moba_kernel(qf_ref, kf_ref, q_ref, k_ref, vt_ref, out_ref, kmean_ref, x_ref):
    nkb = kmean_ref.shape[0]
    for i in range(nkb):
        kb = k_ref[i * MOBA_BLOCK:(i + 1) * MOBA_BLOCK, :].astype(jnp.float32)
        kmean_ref[i:i + 1, :] = jnp.sum(kb, axis=0, keepdims=True) * (1.0 / MOBA_BLOCK)
    _sweep(_moba_units(pl.program_id(1), qf_ref, q_ref, kmean_ref), k_ref, kf_ref, vt_ref, out_ref, x_ref)


def _score_scratch(seq):
    assert SCORE_SLOTS == 4
    return pltpu.VMEM((2, 2, seq, MOBA_BLOCK), jnp.float32)


def _attn_vmem_estimate(seq):
    blocks = 2 * 5 * seq * PAIR * 2
    temps = SCORE_SLOTS * seq * MOBA_BLOCK * 4 + 8 * MOBA_BLOCK * MOBA_BLOCK * 4
    return blocks + temps


def _moba_attention(qk, vt, qfeat, kfeat, batch, seq, d):
    nq = seq // MOBA_BLOCK
    npair = d // PAIR
    est = _attn_vmem_estimate(seq) + qfeat.size * 4
    return pl.pallas_call(
        _moba_kernel,
        grid=(batch, npair),
        in_specs=[
            _const_spec(qfeat.shape),
            _const_spec(kfeat.shape),
            pl.BlockSpec((seq, PAIR), lambda b, p: (b, p)),
            pl.BlockSpec((seq, PAIR), lambda b, p: (b, npair + p)),
            pl.BlockSpec((1, PAIR, seq), lambda b, p: (b, p, 0)),
        ],
        out_specs=pl.BlockSpec((seq, PAIR), lambda b, p: (b, p)),
        out_shape=jax.ShapeDtypeStruct((batch * seq, d), jnp.bfloat16),
        scratch_shapes=[pltpu.VMEM((nq, PAIR), jnp.float32), _score_scratch(seq)],
        compiler_params=pltpu.CompilerParams(
            dimension_semantics=("parallel", "parallel"), vmem_limit_bytes=_vmem_limit(est)),
        name="moba_attention",
    )(qfeat, kfeat, qk, qk, vt)


def _fox_units(q_ref):
    for jj in range(q_ref.shape[0] // MOBA_BLOCK):
        q2 = q_ref[jj * MOBA_BLOCK:(jj + 1) * MOBA_BLOCK, :]
        lane = lax.broadcasted_iota(jnp.int32, q2.shape, 1)
        for a in range(2):
            qfeat = jnp.where((lane >= a * N_PIECES) & (lane < (a + 1) * N_PIECES), 1.0, 0.0).astype(jnp.bfloat16)
            yield _Unit(jj, a, jnp.concatenate([_head_mask(q2, a), qfeat], axis=1), lambda i: None)


def _fox_kernel(q_ref, k_ref, kf_ref, vt_ref, out_ref, x_ref):
    _sweep(_fox_units(q_ref), k_ref, kf_ref, vt_ref, out_ref, x_ref)


def _fox_attention(q, k, kfeat, vt, batch, seq, d):
    npair = d // PAIR
    est = _attn_vmem_estimate(seq)
    return pl.pallas_call(
        _fox_kernel,
        grid=(batch, npair),
        in_specs=[
            pl.BlockSpec((seq, PAIR), lambda b, p: (b, p)),
            pl.BlockSpec((seq, PAIR), lambda b, p: (b, p)),
            pl.BlockSpec((seq, PAIR), lambda b, p: (b, p)),
            pl.BlockSpec((1, PAIR, seq), lambda b, p: (b, p, 0)),
        ],
        out_specs=pl.BlockSpec((seq, PAIR), lambda b, p: (b, p)),
        out_shape=jax.ShapeDtypeStruct((batch * seq, d), jnp.bfloat16),
        scratch_shapes=[_score_scratch(seq)],
        compiler_params=pltpu.CompilerParams(
            dimension_semantics=("parallel", "parallel"), vmem_limit_bytes=_vmem_limit(est)),
        name="fox_attention",
    )(q, k, kfeat, vt)


def _np_bf16_pieces(x):
    out = []
    r = np.asarray(x, np.float64)
    for _ in range(N_PIECES):
        p = r.astype(np.float32).astype(jnp.bfloat16).astype(np.float64)
        out.append(p.astype(np.float32))
        r = r - p
    return out


def _moba_features(nh, seq):
    slopes = (2.0 ** (-8.0 * np.arange(1, nh + 1) / nh)).astype(np.float32).astype(np.float64) * LOG2E
    qfeat = np.zeros((nh, PAIR), np.float32)
    kfeat = np.zeros((seq, PAIR), np.float32)
    pos = np.arange(seq)
    for k, piece in enumerate(_np_bf16_pieces(slopes)):
        qfeat[:, k] = piece
        qfeat[:, N_PIECES + k] = piece
        kfeat[:, k] = pos % MOBA_BLOCK
        kfeat[:, N_PIECES + k] = pos - pos % MOBA_BLOCK
    return jnp.asarray(qfeat), jnp.asarray(kfeat, jnp.bfloat16)


def _fox_placement(nh, d):
    place = np.zeros((N_PIECES * nh, d), np.float32)
    for h in range(nh):
        for k in range(N_PIECES):
            place[k * nh + h, (h // 2) * PAIR + (h % 2) * N_PIECES + k] = 1.0
    return jnp.asarray(place, jnp.bfloat16)


def kernel(x, attn_norm, moba_w_qkv, fox_w_q, w_o, kv_norm, w_kv, w_f, b_f, mlp_norm, w_up, w_down, final_norm):
    batch, seq, d = x.shape
    depth = attn_norm.shape[0]
    n_a = moba_w_qkv.shape[0]
    nh = w_f.shape[1]
    assert d == nh * HEAD_DIM and d % PAIR == 0 and seq % MOBA_BLOCK == 0
    assert seq // MOBA_BLOCK - 1 >= MOBA_TOPK

    bf16 = jnp.bfloat16
    q_scale = LOG2E / math.sqrt(HEAD_DIM)
    moba_qfeat, moba_kfeat = _moba_features(nh, seq)

    h = x.reshape(batch * seq, d)
    row = lambda v: v.reshape(1, -1)
    k_sh = vt_sh = kf_sh = None
    for layer in range(depth):
        if layer == n_a:
            k_sh, vt_sh, kf_sh = _kv_proj(
                h, row(kv_norm), w_kv[:, :d].astype(bf16), w_kv[:, d:].T.astype(bf16),
                jnp.tile(w_f, (1, N_PIECES)).astype(bf16), row(jnp.tile(b_f, N_PIECES)),
                _fox_placement(nh, d), batch, seq)
        if layer < n_a:
            w = moba_w_qkv[layer]
            qk, vt = _qkv_proj(h, row(attn_norm[layer]), w[:, :2 * d].astype(bf16),
                               w[:, 2 * d:].T.astype(bf16), batch, seq, q_scale)
            mix = _moba_attention(qk, vt, moba_qfeat, moba_kfeat, batch, seq, d)
        else:
            q = _q_proj(h, row(attn_norm[layer]), fox_w_q[layer - n_a].astype(bf16), q_scale)
            mix = _fox_attention(q, k_sh, kf_sh, vt_sh, batch, seq, d)
        h = _attn_out_mlp(h, mix, w_o[layer].astype(bf16), row(mlp_norm[layer]),
                          w_up[layer].astype(bf16), w_down[layer].astype(bf16),
                          row(final_norm), final_norm=(layer == depth - 1))
    return h.reshape(batch, seq, d)
```

```python
import functools
import math

import numpy as np
import jax
import jax.numpy as jnp
from jax import lax
from jax.experimental import pallas as pl
from jax.experimental.pallas import tpu as pltpu

RMS_EPS = 1e-6
NEG_INF = -1e30
LOG2E = 1.4426950408889634
MOBA_BLOCK = 256
MOBA_TOPK = 3
HEAD_DIM = 64
PAIR = 2 * HEAD_DIM
V7X_VMEM_BYTES = 64 * 1024 * 1024
ROW_TILE = 512
FF_CHUNK = 1024
N_PIECES = 3
BF16_ROWS = 16
ONES_ROWS = BF16_ROWS
SCORE_SLOTS = 6
LOOKAHEAD = SCORE_SLOTS - 1

_NT = (((1,), (1,)), ((), ()))


def _vmem_limit(nbytes):
    return int(min(max(nbytes * 3 // 2, 16 * 1024 * 1024), V7X_VMEM_BYTES * 7 // 8))


def _rms(x, g):
    return x * lax.rsqrt(jnp.mean(x * x, axis=-1, keepdims=True) + RMS_EPS) * g


def _const_spec(shape):
    return pl.BlockSpec(shape, lambda *_: (0,) * len(shape), pipeline_mode=pl.Buffered(1))


def _first_step():
    return (pl.program_id(0) == 0) & (pl.program_id(1) == 0)


def _transpose_weight(wt_ref, w):
    wt_ref[...] = w.astype(jnp.float32).T.astype(wt_ref.dtype)


def _qkv_kernel(h_ref, g_ref, w_ref, qk_ref, vt_ref, wvt_ref, *, q_scale):
    d = h_ref.shape[1]

    @pl.when(_first_step())
    def _():
        _transpose_weight(wvt_ref, w_ref[:, 2 * d:])

    xn = _rms(h_ref[...], g_ref[...]).astype(jnp.bfloat16)
    q = jnp.dot(xn, w_ref[:, :d], preferred_element_type=jnp.float32)
    qk_ref[:, :d] = (q * q_scale).astype(qk_ref.dtype)
    k = jnp.dot(xn, w_ref[:, d:2 * d], preferred_element_type=jnp.float32)
    qk_ref[:, d:] = k.astype(qk_ref.dtype)
    vt = lax.dot_general(wvt_ref[...], xn, _NT, preferred_element_type=jnp.float32)
    vt_ref[0] = vt.astype(vt_ref.dtype)


def _qkv_proj(h, g, w, batch, seq, q_scale):
    n, d = h.shape
    tm = min(ROW_TILE, seq)
    nt = seq // tm
    est = 2 * tm * d * 4 + (w.size + d * d) * 2 + 2 * (tm * 2 * d + d * tm) * 2 + 4 * tm * d * 4 + d * d * 4
    return pl.pallas_call(
        functools.partial(_qkv_kernel, q_scale=q_scale),
        grid=(batch, nt),
        in_specs=[
            pl.BlockSpec((tm, d), lambda b, t: (b * nt + t, 0)),
            _const_spec((1, d)),
            _const_spec(w.shape),
        ],
        out_specs=[
            pl.BlockSpec((tm, 2 * d), lambda b, t: (b * nt + t, 0)),
            pl.BlockSpec((1, d, tm), lambda b, t: (b, 0, t)),
        ],
        out_shape=[
            jax.ShapeDtypeStruct((n, 2 * d), jnp.bfloat16),
            jax.ShapeDtypeStruct((batch, d, seq), jnp.bfloat16),
        ],
        scratch_shapes=[pltpu.VMEM((d, d), jnp.bfloat16)],
        compiler_params=pltpu.CompilerParams(
            dimension_semantics=("arbitrary", "arbitrary"), vmem_limit_bytes=_vmem_limit(est)),
        name="moba_qkv_proj",
    )(h, g, w)


def _q_kernel(h_ref, g_ref, wq_ref, q_ref, *, q_scale):
    xn = _rms(h_ref[...], g_ref[...]).astype(jnp.bfloat16)
    q = jnp.dot(xn, wq_ref[...], preferred_element_type=jnp.float32)
    q_ref[...] = (q * q_scale).astype(q_ref.dtype)


def _q_proj(h, g, wq, q_scale):
    n, d = h.shape
    tm = min(ROW_TILE, n)
    est = 2 * tm * d * 4 + wq.size * 2 + 2 * tm * d * 2 + 3 * tm * d * 4
    return pl.pallas_call(
        functools.partial(_q_kernel, q_scale=q_scale),
        grid=(n // tm,),
        in_specs=[pl.BlockSpec((tm, d), lambda r: (r, 0)), _const_spec((1, d)), _const_spec(wq.shape)],
        out_specs=pl.BlockSpec((tm, d), lambda r: (r, 0)),
        out_shape=jax.ShapeDtypeStruct((n, d), jnp.bfloat16),
        compiler_params=pltpu.CompilerParams(
            dimension_semantics=("parallel",), vmem_limit_bytes=_vmem_limit(est)),
        name="fox_q_proj",
    )(h, g, wq)


def _bf16_pieces(x):
    p1 = x.astype(jnp.bfloat16)
    r1 = x - p1.astype(jnp.float32)
    p2 = r1.astype(jnp.bfloat16)
    p3 = (r1 - p2.astype(jnp.float32)).astype(jnp.bfloat16)
    return p1, p2, p3


def _kv_kernel(h_ref, g_ref, w_ref, wf_ref, bf_ref, place_ref, k_ref, vt_ref, kf_ref, carry_ref, wvt_ref):
    tm, d = h_ref.shape
    nh = wf_ref.shape[1] // N_PIECES

    @pl.when(_first_step())
    def _():
        _transpose_weight(wvt_ref, w_ref[:, d:])

    @pl.when(pl.program_id(1) == 0)
    def _():
        carry_ref[...] = jnp.zeros_like(carry_ref)

    xn = _rms(h_ref[...], g_ref[...]).astype(jnp.bfloat16)
    k_ref[...] = jnp.dot(xn, w_ref[:, :d], preferred_element_type=jnp.float32).astype(k_ref.dtype)
    vt = lax.dot_general(wvt_ref[...], xn, _NT, preferred_element_type=jnp.float32)
    vt_ref[0] = vt.astype(vt_ref.dtype)

    z = jnp.dot(xn, wf_ref[...], preferred_element_type=jnp.float32) + bf_ref[...]
    log_f = jnp.minimum(z, 0.0) - jnp.log1p(jnp.exp(-jnp.abs(z)))
    row = lax.broadcasted_iota(jnp.int32, (tm, tm), 0)
    col = lax.broadcasted_iota(jnp.int32, (tm, tm), 1)
    tri = jnp.where(col <= row, 1.0, 0.0).astype(jnp.bfloat16)
    c = carry_ref[...]
    for piece in _bf16_pieces(log_f):
        c = c + jnp.dot(tri, piece, preferred_element_type=jnp.float32)
    carry_ref[...] = c[tm - 1:tm, :]
    p1, p2, p3 = _bf16_pieces(c * (-LOG2E))
    col_f = lax.broadcasted_iota(jnp.int32, c.shape, 1)
    pcs = jnp.where(col_f < nh, p1, jnp.where(col_f < 2 * nh, p2, p3))
    kf = jnp.dot(pcs, place_ref[...], preferred_element_type=jnp.float32)
    kf_ref[...] = kf.astype(kf_ref.dtype)


def _kv_proj(h, g, w, wf, bf, place, batch, seq):
    n, d = h.shape
    nf = wf.shape[1]
    tm = min(ROW_TILE, seq)
    nt = seq // tm
    est = 2 * tm * d * 4 + (w.size + d * d) * 2 + 6 * tm * d * 2 + 5 * tm * d * 4 + tm * tm * 8 + d * d * 4
    return pl.pallas_call(
        _kv_kernel,
        grid=(batch, nt),
        in_specs=[
            pl.BlockSpec((tm, d), lambda b, t: (b * nt + t, 0)),
            _const_spec((1, d)),
            _const_spec(w.shape),
            _const_spec(wf.shape),
            _const_spec((1, nf)),
            _const_spec(place.shape),
        ],
        out_specs=[
            pl.BlockSpec((tm, d), lambda b, t: (b * nt + t, 0)),
            pl.BlockSpec((1, d, tm), lambda b, t: (b, 0, t)),
            pl.BlockSpec((tm, d), lambda b, t: (b * nt + t, 0)),
        ],
        out_shape=[
            jax.ShapeDtypeStruct((n, d), jnp.bfloat16),
            jax.ShapeDtypeStruct((batch, d, seq), jnp.bfloat16),
            jax.ShapeDtypeStruct((n, d), jnp.bfloat16),
        ],
        scratch_shapes=[pltpu.VMEM((1, nf), jnp.float32), pltpu.VMEM((d, d), jnp.bfloat16)],
        compiler_params=pltpu.CompilerParams(
            dimension_semantics=("arbitrary", "arbitrary"), vmem_limit_bytes=_vmem_limit(est)),
        name="shared_kv_proj",
    )(h, g, w, wf, bf, place)


def _mlp_kernel(h_ref, mix_ref, wo_ref, g_ref, wup_ref, wdn_ref, gf_ref, out_ref, *, final_norm):
    ff = wup_ref.shape[1]
    fc = min(FF_CHUNK, ff)
    h1 = h_ref[...] + jnp.dot(mix_ref[...], wo_ref[...], preferred_element_type=jnp.float32)
    xn = _rms(h1, g_ref[...]).astype(jnp.bfloat16)
    acc = h1
    for c in range(ff // fc):
        u = jnp.dot(xn, wup_ref[:, c * fc:(c + 1) * fc], preferred_element_type=jnp.float32)
        a = jnp.square(jnp.maximum(u, 0.0)).astype(jnp.bfloat16)
        acc = acc + jnp.dot(a, wdn_ref[c * fc:(c + 1) * fc, :], preferred_element_type=jnp.float32)
    if final_norm:
        acc = _rms(acc, gf_ref[...])
    out_ref[...] = acc


def _attn_out_mlp(h, mix, wo, g, wup, wdn, gf, final_norm):
    n, d = h.shape
    ff = wup.shape[1]
    tm = min(ROW_TILE, n)
    fc = min(FF_CHUNK, ff)
    est = (4 * tm * d * 4 + 2 * tm * d * 2 + (wo.size + wup.size + wdn.size) * 2
           + 3 * tm * d * 4 + tm * fc * 6)
    return pl.pallas_call(
        functools.partial(_mlp_kernel, final_norm=final_norm),
        grid=(n // tm,),
        in_specs=[
            pl.BlockSpec((tm, d), lambda r: (r, 0)),
            pl.BlockSpec((tm, d), lambda r: (r, 0)),
            _const_spec(wo.shape),
            _const_spec((1, d)),
            _const_spec(wup.shape),
            _const_spec(wdn.shape),
            _const_spec((1, d)),
        ],
        out_specs=pl.BlockSpec((tm, d), lambda r: (r, 0)),
        out_shape=jax.ShapeDtypeStruct((n, d), jnp.float32),
        compiler_params=pltpu.CompilerParams(
            dimension_semantics=("parallel",), vmem_limit_bytes=_vmem_limit(est)),
        name="attn_out_mlp",
    )(h, mix, wo, g, wup, wdn, gf)


def _head_mask(q2, a):
    lane = lax.broadcasted_iota(jnp.int32, q2.shape, 1)
    keep = (lane < HEAD_DIM) if a == 0 else (lane >= HEAD_DIM)
    return jnp.where(keep, q2, jnp.zeros_like(q2))


class _Unit:
    def __init__(self, jj, a, q_ext, shift):
        self.jj, self.a, self.q_ext = jj, a, q_ext
        self.nblk = jj + 1
        self.shifts = [shift(i) for i in range(jj)] + [None]
        self.m = None
        self.o = None


def _score_tile(u, i, k_ref, kf_ref, x_ref):
    tq = u.q_ext.shape[0]
    rows = slice(i * MOBA_BLOCK, (i + 1) * MOBA_BLOCK)
    k_ext = jnp.concatenate([k_ref[rows, :], kf_ref[rows, :]], axis=1)
    x = lax.dot_general(k_ext, u.q_ext, _NT, preferred_element_type=jnp.float32)
    if i == u.nblk - 1:
        key_l = lax.broadcasted_iota(jnp.int32, (MOBA_BLOCK, tq), 0)
        qry_l = lax.broadcasted_iota(jnp.int32, (MOBA_BLOCK, tq), 1)
        x = jnp.where(key_l <= qry_l, x, NEG_INF)
    mi = jnp.max(x, axis=0, keepdims=True)
    if u.shifts[i] is not None:
        mi = mi + u.shifts[i]
    u.m = mi if u.m is None else jnp.maximum(u.m, mi)
    x_ref[u.jj % (SCORE_SLOTS // 2), u.a, rows, :] = x


def _value_tile(u, i, vt_ref, x_ref):
    rows = slice(i * MOBA_BLOCK, (i + 1) * MOBA_BLOCK)
    mm = u.m if u.shifts[i] is None else u.m - u.shifts[i]
    pt = jnp.exp2(x_ref[u.jj % (SCORE_SLOTS // 2), u.a, rows, :] - mm).astype(jnp.bfloat16)
    ones = jnp.ones((ONES_ROWS, MOBA_BLOCK), jnp.bfloat16)
    v_ext = jnp.concatenate([vt_ref[0, u.a * HEAD_DIM:(u.a + 1) * HEAD_DIM, rows], ones], axis=0)
    oi = jnp.dot(v_ext, pt, preferred_element_type=jnp.float32)
    u.o = oi if u.o is None else u.o + oi


def _sweep(units, k_ref, kf_ref, vt_ref, out_ref, x_ref):
    units = list(units)
    for u in units[:LOOKAHEAD]:
        for i in range(u.nblk):
            _score_tile(u, i, k_ref, kf_ref, x_ref)
    done = {}
    for n, u in enumerate(units):
        nxt = units[n + LOOKAHEAD] if n + LOOKAHEAD < len(units) else None
        for i in range(max(u.nblk, nxt.nblk if nxt else 0)):
            if nxt is not None and i < nxt.nblk:
                _score_tile(nxt, i, k_ref, kf_ref, x_ref)
            if i < u.nblk:
                _value_tile(u, i, vt_ref, x_ref)
        done.setdefault(u.jj, []).append(u.o[:HEAD_DIM, :] * (1.0 / u.o[HEAD_DIM:HEAD_DIM + 1, :]))
        if len(done[u.jj]) == 2:
            ot = jnp.concatenate(done.pop(u.jj), axis=0)
            out_ref[u.jj * MOBA_BLOCK:(u.jj + 1) * MOBA_BLOCK, :] = ot.T.astype(out_ref.dtype)


def _moba_units(p, qf_ref, q_ref, kmean_ref, nkb):
    tq = MOBA_BLOCK
    km_rows = kmean_ref.shape[0]
    km_pieces = jnp.concatenate(_bf16_pieces(kmean_ref[...]), axis=0)
    for jj in range(nkb):
        q2 = q_ref[jj * MOBA_BLOCK:(jj + 1) * MOBA_BLOCK, :]
        for a in range(2):
            qa = _head_mask(q2, a)
            qfeat = jnp.broadcast_to(qf_ref[pl.ds(2 * p + a, 1), :], (tq, PAIR)).astype(jnp.bfloat16)
            if jj > MOBA_TOPK:
                g = lax.dot_general(km_pieces, qa, _NT, preferred_element_type=jnp.float32)
                gate = sum(g[k * km_rows:k * km_rows + nkb, :] for k in range(N_PIECES))
                blk = lax.broadcasted_iota(jnp.int32, (nkb, tq), 0)
                beaten = jnp.zeros((nkb, tq), jnp.float32)
                for ip in range(jj):
                    row = gate[ip:ip + 1, :]
                    beaten = beaten + jnp.where(row > gate, 1.0, jnp.where((row == gate) & (blk > ip), 1.0, 0.0))
                unsel = jnp.where(beaten < float(MOBA_TOPK), 0.0, NEG_INF)
                shift = lambda i, unsel=unsel: unsel[i:i + 1, :]
            else:
                shift = lambda i: None
            yield _Unit(jj, a, jnp.concatenate([qa, qfeat], axis=1), shift)


def _moba_kernel(qf_ref, kf_ref, q_ref, k_ref, vt_ref, out_ref, kmean_ref, x_ref):
    nkb = q_ref.shape[0] // MOBA_BLOCK
    kmean_ref[...] = jnp.zeros_like(kmean_ref)
    for i in range(nkb):
        kb = k_ref[i * MOBA_BLOCK:(i + 1) * MOBA_BLOCK, :].astype(jnp.float32)
        kmean_ref[i:i + 1, :] = jnp.sum(kb, axis=0, keepdims=True) * (1.0 / MOBA_BLOCK)
    _sweep(_moba_units(pl.program_id(1), qf_ref, q_ref, kmean_ref, nkb), k_ref, kf_ref, vt_ref, out_ref, x_ref)


def _score_scratch(seq):
    return pltpu.VMEM((SCORE_SLOTS // 2, 2, seq, MOBA_BLOCK), jnp.float32)


def _attn_vmem_estimate(seq):
    blocks = 2 * 5 * seq * PAIR * 2
    temps = SCORE_SLOTS * seq * MOBA_BLOCK * 4 + 8 * MOBA_BLOCK * MOBA_BLOCK * 4
    return blocks + temps


def _moba_attention(qk, vt, qfeat, kfeat, batch, seq, d):
    nq = seq // MOBA_BLOCK
    npair = d // PAIR
    est = _attn_vmem_estimate(seq) + qfeat.size * 4
    return pl.pallas_call(
        _moba_kernel,
        grid=(batch, npair),
        in_specs=[
            _const_spec(qfeat.shape),
            _const_spec(kfeat.shape),
            pl.BlockSpec((seq, PAIR), lambda b, p: (b, p)),
            pl.BlockSpec((seq, PAIR), lambda b, p: (b, npair + p)),
            pl.BlockSpec((1, PAIR, seq), lambda b, p: (b, p, 0)),
        ],
        out_specs=pl.BlockSpec((seq, PAIR), lambda b, p: (b, p)),
        out_shape=jax.ShapeDtypeStruct((batch * seq, d), jnp.bfloat16),
        scratch_shapes=[pltpu.VMEM((-(-nq // BF16_ROWS) * BF16_ROWS, PAIR), jnp.float32), _score_scratch(seq)],
        compiler_params=pltpu.CompilerParams(
            dimension_semantics=("parallel", "parallel"), vmem_limit_bytes=_vmem_limit(est)),
        name="moba_attention",
    )(qfeat, kfeat, qk, qk, vt)


def _fox_units(q_ref):
    for jj in range(q_ref.shape[0] // MOBA_BLOCK):
        q2 = q_ref[jj * MOBA_BLOCK:(jj + 1) * MOBA_BLOCK, :]
        lane = lax.broadcasted_iota(jnp.int32, q2.shape, 1)
        for a in range(2):
            qfeat = jnp.where((lane >= a * N_PIECES) & (lane < (a + 1) * N_PIECES), 1.0, 0.0).astype(jnp.bfloat16)
            yield _Unit(jj, a, jnp.concatenate([_head_mask(q2, a), qfeat], axis=1), lambda i: None)


def _fox_kernel(q_ref, k_ref, kf_ref, vt_ref, out_ref, x_ref):
    _sweep(_fox_units(q_ref), k_ref, kf_ref, vt_ref, out_ref, x_ref)


def _fox_attention(q, k, kfeat, vt, batch, seq, d):
    npair = d // PAIR
    est = _attn_vmem_estimate(seq)
    return pl.pallas_call(
        _fox_kernel,
        grid=(batch, npair),
        in_specs=[
            pl.BlockSpec((seq, PAIR), lambda b, p: (b, p)),
            pl.BlockSpec((seq, PAIR), lambda b, p: (b, p)),
            pl.BlockSpec((seq, PAIR), lambda b, p: (b, p)),
            pl.BlockSpec((1, PAIR, seq), lambda b, p: (b, p, 0)),
        ],
        out_specs=pl.BlockSpec((seq, PAIR), lambda b, p: (b, p)),
        out_shape=jax.ShapeDtypeStruct((batch * seq, d), jnp.bfloat16),
        scratch_shapes=[_score_scratch(seq)],
        compiler_params=pltpu.CompilerParams(
            dimension_semantics=("parallel", "parallel"), vmem_limit_bytes=_vmem_limit(est)),
        name="fox_attention",
    )(q, k, kfeat, vt)


def _np_bf16_pieces(x):
    out = []
    r = np.asarray(x, np.float64)
    for _ in range(N_PIECES):
        p = r.astype(np.float32).astype(jnp.bfloat16).astype(np.float64)
        out.append(p.astype(np.float32))
        r = r - p
    return out


def _moba_features(nh, seq):
    slopes = (2.0 ** (-8.0 * np.arange(1, nh + 1) / nh)).astype(np.float32).astype(np.float64) * LOG2E
    qfeat = np.zeros((nh, PAIR), np.float32)
    kfeat = np.zeros((seq, PAIR), np.float32)
    pos = np.arange(seq)
    for k, piece in enumerate(_np_bf16_pieces(slopes)):
        qfeat[:, k] = piece
        qfeat[:, N_PIECES + k] = piece
        kfeat[:, k] = pos % MOBA_BLOCK
        kfeat[:, N_PIECES + k] = pos - pos % MOBA_BLOCK
    return jnp.asarray(qfeat), jnp.asarray(kfeat, jnp.bfloat16)


def _fox_placement(nh, d):
    place = np.zeros((N_PIECES * nh, d), np.float32)
    for h in range(nh):
        for k in range(N_PIECES):
            place[k * nh + h, (h // 2) * PAIR + (h % 2) * N_PIECES + k] = 1.0
    return jnp.asarray(place, jnp.bfloat16)


def kernel(x, attn_norm, moba_w_qkv, fox_w_q, w_o, kv_norm, w_kv, w_f, b_f, mlp_norm, w_up, w_down, final_norm):
    batch, seq, d = x.shape
    depth = attn_norm.shape[0]
    n_a = moba_w_qkv.shape[0]
    nh = w_f.shape[1]
    assert d == nh * HEAD_DIM and d % PAIR == 0 and seq % MOBA_BLOCK == 0
    assert seq // MOBA_BLOCK - 1 >= MOBA_TOPK

    bf16 = jnp.bfloat16
    q_scale = LOG2E / math.sqrt(HEAD_DIM)
    moba_qfeat, moba_kfeat = _moba_features(nh, seq)

    h = x.reshape(batch * seq, d)
    row = lambda v: v.reshape(1, -1)
    k_sh = vt_sh = kf_sh = None
    for layer in range(depth):
        if layer == n_a:
            k_sh, vt_sh, kf_sh = _kv_proj(
                h, row(kv_norm), w_kv.astype(bf16), jnp.tile(w_f, (1, N_PIECES)).astype(bf16),
                row(jnp.tile(b_f, N_PIECES)), _fox_placement(nh, d), batch, seq)
        if layer < n_a:
            qk, vt = _qkv_proj(h, row(attn_norm[layer]), moba_w_qkv[layer].astype(bf16), batch, seq, q_scale)
            mix = _moba_attention(qk, vt, moba_qfeat, moba_kfeat, batch, seq, d)
        else:
            q = _q_proj(h, row(attn_norm[layer]), fox_w_q[layer - n_a].astype(bf16), q_scale)
            mix = _fox_attention(q, k_sh, kf_sh, vt_sh, batch, seq, d)
        h = _attn_out_mlp(h, mix, w_o[layer].astype(bf16), row(mlp_norm[layer]),
                          w_up[layer].astype(bf16), w_down[layer].astype(bf16),
                          row(final_norm), final_norm=(layer == depth - 1))
    return h.reshape(batch, seq, d)
```

```python
import functools
import math

import numpy as np
import jax
import jax.numpy as jnp
from jax import lax
from jax.experimental import pallas as pl
from jax.experimental.pallas import tpu as pltpu

RMS_EPS = 1e-6
NEG_INF = -1e30
LOG2E = 1.4426950408889634
MOBA_BLOCK = 256
MOBA_TOPK = 3
HEAD_DIM = 64
PAIR = 2 * HEAD_DIM
V7X_VMEM_BYTES = 64 * 1024 * 1024
ROW_TILE = 512
FF_CHUNK = 1024
N_PIECES = 3
BF16_ROWS = 16
ONES_ROWS = BF16_ROWS
SCORE_SLOTS = 8
LOOKAHEAD = SCORE_SLOTS - 1

_NT = (((1,), (1,)), ((), ()))


def _vmem_limit(nbytes):
    return int(min(max(nbytes * 3 // 2, 16 * 1024 * 1024), V7X_VMEM_BYTES * 7 // 8))


def _rms(x, g):
    return x * lax.rsqrt(jnp.mean(x * x, axis=-1, keepdims=True) + RMS_EPS) * g


def _const_spec(shape):
    return pl.BlockSpec(shape, lambda *_: (0,) * len(shape), pipeline_mode=pl.Buffered(1))


def _layer_spec(stacked, layer):
    return pl.BlockSpec((None,) + stacked.shape[1:], lambda *_: (layer, 0, 0), pipeline_mode=pl.Buffered(1))


def _first_step():
    return (pl.program_id(0) == 0) & (pl.program_id(1) == 0)


def _transpose_weight(wt_ref, w):
    wt_ref[...] = w.astype(jnp.float32).T.astype(wt_ref.dtype)


def _qkv_kernel(h_ref, g_ref, w_ref, qk_ref, vt_ref, wvt_ref, *, q_scale):
    d = h_ref.shape[1]

    @pl.when(_first_step())
    def _():
        _transpose_weight(wvt_ref, w_ref[:, 2 * d:])

    xn = _rms(h_ref[...], g_ref[...]).astype(jnp.bfloat16)
    q = jnp.dot(xn, w_ref[:, :d], preferred_element_type=jnp.float32)
    qk_ref[:, :d] = (q * q_scale).astype(qk_ref.dtype)
    k = jnp.dot(xn, w_ref[:, d:2 * d], preferred_element_type=jnp.float32)
    qk_ref[:, d:] = k.astype(qk_ref.dtype)
    vt = lax.dot_general(wvt_ref[...], xn, _NT, preferred_element_type=jnp.float32)
    vt_ref[0] = vt.astype(vt_ref.dtype)


def _qkv_proj(h, norms, weights, layer, batch, seq, q_scale):
    n, d = h.shape
    tm = min(ROW_TILE, seq)
    nt = seq // tm
    est = 2 * tm * d * 4 + (weights[0].size + d * d) * 2 + 2 * (tm * 2 * d + d * tm) * 2 + 4 * tm * d * 4 + d * d * 4
    return pl.pallas_call(
        functools.partial(_qkv_kernel, q_scale=q_scale),
        grid=(batch, nt),
        in_specs=[
            pl.BlockSpec((tm, d), lambda b, t: (b * nt + t, 0)),
            _layer_spec(norms, layer),
            _layer_spec(weights, layer),
        ],
        out_specs=[
            pl.BlockSpec((tm, 2 * d), lambda b, t: (b * nt + t, 0)),
            pl.BlockSpec((1, d, tm), lambda b, t: (b, 0, t)),
        ],
        out_shape=[
            jax.ShapeDtypeStruct((n, 2 * d), jnp.bfloat16),
            jax.ShapeDtypeStruct((batch, d, seq), jnp.bfloat16),
        ],
        scratch_shapes=[pltpu.VMEM((d, d), jnp.bfloat16)],
        compiler_params=pltpu.CompilerParams(
            dimension_semantics=("arbitrary", "arbitrary"), vmem_limit_bytes=_vmem_limit(est)),
        name="moba_qkv_proj",
    )(h, norms, weights)


def _q_kernel(h_ref, g_ref, wq_ref, q_ref, *, q_scale):
    xn = _rms(h_ref[...], g_ref[...]).astype(jnp.bfloat16)
    q = jnp.dot(xn, wq_ref[...], preferred_element_type=jnp.float32)
    q_ref[...] = (q * q_scale).astype(q_ref.dtype)


def _q_proj(h, norms, weights, norm_layer, layer, q_scale):
    n, d = h.shape
    tm = min(ROW_TILE, n)
    est = 2 * tm * d * 4 + weights[0].size * 2 + 2 * tm * d * 2 + 3 * tm * d * 4
    return pl.pallas_call(
        functools.partial(_q_kernel, q_scale=q_scale),
        grid=(n // tm,),
        in_specs=[pl.BlockSpec((tm, d), lambda r: (r, 0)), _layer_spec(norms, norm_layer),
                  _layer_spec(weights, layer)],
        out_specs=pl.BlockSpec((tm, d), lambda r: (r, 0)),
        out_shape=jax.ShapeDtypeStruct((n, d), jnp.bfloat16),
        compiler_params=pltpu.CompilerParams(
            dimension_semantics=("parallel",), vmem_limit_bytes=_vmem_limit(est)),
        name="fox_q_proj",
    )(h, norms, weights)


def _bf16_pieces(x):
    p1 = x.astype(jnp.bfloat16)
    r1 = x - p1.astype(jnp.float32)
    p2 = r1.astype(jnp.bfloat16)
    p3 = (r1 - p2.astype(jnp.float32)).astype(jnp.bfloat16)
    return p1, p2, p3


def _kv_kernel(h_ref, g_ref, w_ref, wf_ref, bf_ref, place_ref, k_ref, vt_ref, kf_ref, carry_ref, wvt_ref):
    tm, d = h_ref.shape
    nh = wf_ref.shape[1] // N_PIECES

    @pl.when(_first_step())
    def _():
        _transpose_weight(wvt_ref, w_ref[:, d:])

    @pl.when(pl.program_id(1) == 0)
    def _():
        carry_ref[...] = jnp.zeros_like(carry_ref)

    xn = _rms(h_ref[...], g_ref[...]).astype(jnp.bfloat16)
    k_ref[...] = jnp.dot(xn, w_ref[:, :d], preferred_element_type=jnp.float32).astype(k_ref.dtype)
    vt = lax.dot_general(wvt_ref[...], xn, _NT, preferred_element_type=jnp.float32)
    vt_ref[0] = vt.astype(vt_ref.dtype)

    z = jnp.dot(xn, wf_ref[...], preferred_element_type=jnp.float32) + bf_ref[...]
    log_f = jnp.minimum(z, 0.0) - jnp.log1p(jnp.exp(-jnp.abs(z)))
    row = lax.broadcasted_iota(jnp.int32, (tm, tm), 0)
    col = lax.broadcasted_iota(jnp.int32, (tm, tm), 1)
    tri = jnp.where(col <= row, 1.0, 0.0).astype(jnp.bfloat16)
    c = carry_ref[...]
    for piece in _bf16_pieces(log_f):
        c = c + jnp.dot(tri, piece, preferred_element_type=jnp.float32)
    carry_ref[...] = c[tm - 1:tm, :]
    p1, p2, p3 = _bf16_pieces(c * (-LOG2E))
    col_f = lax.broadcasted_iota(jnp.int32, c.shape, 1)
    pcs = jnp.where(col_f < nh, p1, jnp.where(col_f < 2 * nh, p2, p3))
    kf = jnp.dot(pcs, place_ref[...], preferred_element_type=jnp.float32)
    kf_ref[...] = kf.astype(kf_ref.dtype)


def _kv_proj(h, g, w, wf, bf, place, batch, seq):
    n, d = h.shape
    nf = wf.shape[1]
    tm = min(ROW_TILE, seq)
    nt = seq // tm
    est = 2 * tm * d * 4 + (w.size + d * d) * 2 + 6 * tm * d * 2 + 5 * tm * d * 4 + tm * tm * 8 + d * d * 4
    return pl.pallas_call(
        _kv_kernel,
        grid=(batch, nt),
        in_specs=[
            pl.BlockSpec((tm, d), lambda b, t: (b * nt + t, 0)),
            _const_spec((1, d)),
            _const_spec(w.shape),
            _const_spec(wf.shape),
            _const_spec((1, nf)),
            _const_spec(place.shape),
        ],
        out_specs=[
            pl.BlockSpec((tm, d), lambda b, t: (b * nt + t, 0)),
            pl.BlockSpec((1, d, tm), lambda b, t: (b, 0, t)),
            pl.BlockSpec((tm, d), lambda b, t: (b * nt + t, 0)),
        ],
        out_shape=[
            jax.ShapeDtypeStruct((n, d), jnp.bfloat16),
            jax.ShapeDtypeStruct((batch, d, seq), jnp.bfloat16),
            jax.ShapeDtypeStruct((n, d), jnp.bfloat16),
        ],
        scratch_shapes=[pltpu.VMEM((1, nf), jnp.float32), pltpu.VMEM((d, d), jnp.bfloat16)],
        compiler_params=pltpu.CompilerParams(
            dimension_semantics=("arbitrary", "arbitrary"), vmem_limit_bytes=_vmem_limit(est)),
        name="shared_kv_proj",
    )(h, g, w, wf, bf, place)


def _mlp_kernel(h_ref, mix_ref, wo_ref, g_ref, wup_ref, wdn_ref, gf_ref, out_ref, *, final_norm):
    ff = wup_ref.shape[1]
    fc = min(FF_CHUNK, ff)
    h1 = h_ref[...] + jnp.dot(mix_ref[...], wo_ref[...], preferred_element_type=jnp.float32)
    xn = _rms(h1, g_ref[...]).astype(jnp.bfloat16)
    acc = h1
    for c in range(ff // fc):
        u = jnp.dot(xn, wup_ref[:, c * fc:(c + 1) * fc], preferred_element_type=jnp.float32)
        a = jnp.square(jnp.maximum(u, 0.0)).astype(jnp.bfloat16)
        acc = acc + jnp.dot(a, wdn_ref[c * fc:(c + 1) * fc, :], preferred_element_type=jnp.float32)
    if final_norm:
        acc = _rms(acc, gf_ref[...])
    out_ref[...] = acc


def _attn_out_mlp(h, mix, wo, norms, wup, wdn, gf, layer, final_norm):
    n, d = h.shape
    ff = wup.shape[2]
    tm = min(ROW_TILE, n)
    fc = min(FF_CHUNK, ff)
    est = (4 * tm * d * 4 + 2 * tm * d * 2 + (wo[0].size + wup[0].size + wdn[0].size) * 2
           + 3 * tm * d * 4 + tm * fc * 6)
    return pl.pallas_call(
        functools.partial(_mlp_kernel, final_norm=final_norm),
        grid=(n // tm,),
        in_specs=[
            pl.BlockSpec((tm, d), lambda r: (r, 0)),
            pl.BlockSpec((tm, d), lambda r: (r, 0)),
            _layer_spec(wo, layer),
            _layer_spec(norms, layer),
            _layer_spec(wup, layer),
            _layer_spec(wdn, layer),
            _const_spec((1, d)),
        ],
        out_specs=pl.BlockSpec((tm, d), lambda r: (r, 0)),
        out_shape=jax.ShapeDtypeStruct((n, d), jnp.float32),
        compiler_params=pltpu.CompilerParams(
            dimension_semantics=("parallel",), vmem_limit_bytes=_vmem_limit(est)),
        name="attn_out_mlp",
    )(h, mix, wo, norms, wup, wdn, gf)


def _head_mask(q2, a):
    lane = lax.broadcasted_iota(jnp.int32, q2.shape, 1)
    keep = (lane < HEAD_DIM) if a == 0 else (lane >= HEAD_DIM)
    return jnp.where(keep, q2, jnp.zeros_like(q2))


class _Unit:
    def __init__(self, jj, a, q_ext, shift):
        self.jj, self.a, self.q_ext = jj, a, q_ext
        self.nblk = jj + 1
        self.shifts = [shift(i) for i in range(jj)] + [None]
        self.m = None
        self.o = None


def _score_tile(u, i, k_ref, kf_ref, x_ref):
    tq = u.q_ext.shape[0]
    rows = slice(i * MOBA_BLOCK, (i + 1) * MOBA_BLOCK)
    k_ext = jnp.concatenate([k_ref[rows, :], kf_ref[rows, :]], axis=1)
    x = lax.dot_general(k_ext, u.q_ext, _NT, preferred_element_type=jnp.float32)
    if i == u.nblk - 1:
        key_l = lax.broadcasted_iota(jnp.int32, (MOBA_BLOCK, tq), 0)
        qry_l = lax.broadcasted_iota(jnp.int32, (MOBA_BLOCK, tq), 1)
        x = jnp.where(key_l <= qry_l, x, NEG_INF)
    mi = jnp.max(x, axis=0, keepdims=True)
    if u.shifts[i] is not None:
        mi = mi + u.shifts[i]
    u.m = mi if u.m is None else jnp.maximum(u.m, mi)
    x_ref[u.jj % (SCORE_SLOTS // 2), u.a, rows, :] = x


def _value_tile(u, i, vt_ref, x_ref):
    rows = slice(i * MOBA_BLOCK, (i + 1) * MOBA_BLOCK)
    mm = u.m if u.shifts[i] is None else u.m - u.shifts[i]
    pt = jnp.exp2(x_ref[u.jj % (SCORE_SLOTS // 2), u.a, rows, :] - mm).astype(jnp.bfloat16)
    ones = jnp.ones((ONES_ROWS, MOBA_BLOCK), jnp.bfloat16)
    v_ext = jnp.concatenate([vt_ref[0, u.a * HEAD_DIM:(u.a + 1) * HEAD_DIM, rows], ones], axis=0)
    oi = jnp.dot(v_ext, pt, preferred_element_type=jnp.float32)
    u.o = oi if u.o is None else u.o + oi


def _sweep(units, k_ref, kf_ref, vt_ref, out_ref, x_ref):
    units = list(units)[::-1]
    for u in units[:LOOKAHEAD]:
        for i in range(u.nblk):
            _score_tile(u, i, k_ref, kf_ref, x_ref)
    done = {}
    for n, u in enumerate(units):
        nxt = units[n + LOOKAHEAD] if n + LOOKAHEAD < len(units) else None
        for i in range(max(u.nblk, nxt.nblk if nxt else 0)):
            if nxt is not None and i < nxt.nblk:
                _score_tile(nxt, i, k_ref, kf_ref, x_ref)
            if i < u.nblk:
                _value_tile(u, i, vt_ref, x_ref)
        done.setdefault(u.jj, {})[u.a] = u.o[:HEAD_DIM, :] * (1.0 / u.o[HEAD_DIM:HEAD_DIM + 1, :])
        if len(done[u.jj]) == 2:
            heads = done.pop(u.jj)
            ot = jnp.concatenate([heads[0], heads[1]], axis=0)
            out_ref[u.jj * MOBA_BLOCK:(u.jj + 1) * MOBA_BLOCK, :] = ot.T.astype(out_ref.dtype)


def _moba_units(p, qf_ref, q_ref, kmean_ref, nkb):
    tq = MOBA_BLOCK
    km_rows = kmean_ref.shape[0]
    km_pieces = jnp.concatenate(_bf16_pieces(kmean_ref[...]), axis=0)
    for jj in range(nkb):
        q2 = q_ref[jj * MOBA_BLOCK:(jj + 1) * MOBA_BLOCK, :]
        for a in range(2):
            qa = _head_mask(q2, a)
            qfeat = jnp.broadcast_to(qf_ref[pl.ds(2 * p + a, 1), :], (tq, PAIR)).astype(jnp.bfloat16)
            if jj > MOBA_TOPK:
                g = lax.dot_general(km_pieces, qa, _NT, preferred_element_type=jnp.float32)
                gate = sum(g[k * km_rows:k * km_rows + nkb, :] for k in range(N_PIECES))
                blk = lax.broadcasted_iota(jnp.int32, (nkb, tq), 0)
                beaten = jnp.zeros((nkb, tq), jnp.float32)
                for ip in range(jj):
                    row = gate[ip:ip + 1, :]
                    beaten = beaten + jnp.where(row > gate, 1.0, jnp.where((row == gate) & (blk > ip), 1.0, 0.0))
                unsel = jnp.where(beaten < float(MOBA_TOPK), 0.0, NEG_INF)
                shift = lambda i, unsel=unsel: unsel[i:i + 1, :]
            else:
                shift = lambda i: None
            yield _Unit(jj, a, jnp.concatenate([qa, qfeat], axis=1), shift)


def _moba_kernel(qf_ref, kf_ref, q_ref, k_ref, vt_ref, out_ref, kmean_ref, x_ref):
    nkb = q_ref.shape[0] // MOBA_BLOCK
    kmean_ref[...] = jnp.zeros_like(kmean_ref)
    for i in range(nkb):
        kb = k_ref[i * MOBA_BLOCK:(i + 1) * MOBA_BLOCK, :].astype(jnp.float32)
        kmean_ref[i:i + 1, :] = jnp.sum(kb, axis=0, keepdims=True) * (1.0 / MOBA_BLOCK)
    _sweep(_moba_units(pl.program_id(1), qf_ref, q_ref, kmean_ref, nkb), k_ref, kf_ref, vt_ref, out_ref, x_ref)


def _score_scratch(seq):
    return pltpu.VMEM((SCORE_SLOTS // 2, 2, seq, MOBA_BLOCK), jnp.float32)


def _attn_vmem_estimate(seq):
    blocks = 2 * 5 * seq * PAIR * 2
    temps = SCORE_SLOTS * seq * MOBA_BLOCK * 4 + 8 * MOBA_BLOCK * MOBA_BLOCK * 4
    return blocks + temps


def _moba_attention(qk, vt, qfeat, kfeat, batch, seq, d):
    nq = seq // MOBA_BLOCK
    npair = d // PAIR
    est = _attn_vmem_estimate(seq) + qfeat.size * 4
    return pl.pallas_call(
        _moba_kernel,
        grid=(batch, npair),
        in_specs=[
            _const_spec(qfeat.shape),
            _const_spec(kfeat.shape),
            pl.BlockSpec((seq, PAIR), lambda b, p: (b, p)),
            pl.BlockSpec((seq, PAIR), lambda b, p: (b, npair + p)),
            pl.BlockSpec((1, PAIR, seq), lambda b, p: (b, p, 0)),
        ],
        out_specs=pl.BlockSpec((seq, PAIR), lambda b, p: (b, p)),
        out_shape=jax.ShapeDtypeStruct((batch * seq, d), jnp.bfloat16),
        scratch_shapes=[pltpu.VMEM((-(-nq // BF16_ROWS) * BF16_ROWS, PAIR), jnp.float32), _score_scratch(seq)],
        compiler_params=pltpu.CompilerParams(
            dimension_semantics=("parallel", "parallel"), vmem_limit_bytes=_vmem_limit(est)),
        name="moba_attention",
    )(qfeat, kfeat, qk, qk, vt)


def _fox_units(q_ref):
    for jj in range(q_ref.shape[0] // MOBA_BLOCK):
        q2 = q_ref[jj * MOBA_BLOCK:(jj + 1) * MOBA_BLOCK, :]
        lane = lax.broadcasted_iota(jnp.int32, q2.shape, 1)
        for a in range(2):
            qfeat = jnp.where((lane >= a * N_PIECES) & (lane < (a + 1) * N_PIECES), 1.0, 0.0).astype(jnp.bfloat16)
            yield _Unit(jj, a, jnp.concatenate([_head_mask(q2, a), qfeat], axis=1), lambda i: None)


def _fox_kernel(q_ref, k_ref, kf_ref, vt_ref, out_ref, x_ref):
    _sweep(_fox_units(q_ref), k_ref, kf_ref, vt_ref, out_ref, x_ref)


def _fox_attention(q, k, kfeat, vt, batch, seq, d):
    npair = d // PAIR
    est = _attn_vmem_estimate(seq)
    return pl.pallas_call(
        _fox_kernel,
        grid=(batch, npair),
        in_specs=[
            pl.BlockSpec((seq, PAIR), lambda b, p: (b, p)),
            pl.BlockSpec((seq, PAIR), lambda b, p: (b, p)),
            pl.BlockSpec((seq, PAIR), lambda b, p: (b, p)),
            pl.BlockSpec((1, PAIR, seq), lambda b, p: (b, p, 0)),
        ],
        out_specs=pl.BlockSpec((seq, PAIR), lambda b, p: (b, p)),
        out_shape=jax.ShapeDtypeStruct((batch * seq, d), jnp.bfloat16),
        scratch_shapes=[_score_scratch(seq)],
        compiler_params=pltpu.CompilerParams(
            dimension_semantics=("parallel", "parallel"), vmem_limit_bytes=_vmem_limit(est)),
        name="fox_attention",
    )(q, k, kfeat, vt)


def _np_bf16_pieces(x):
    out = []
    r = np.asarray(x, np.float64)
    for _ in range(N_PIECES):
        p = r.astype(np.float32).astype(jnp.bfloat16).astype(np.float64)
        out.append(p.astype(np.float32))
        r = r - p
    return out


def _moba_features(nh, seq):
    slopes = (2.0 ** (-8.0 * np.arange(1, nh + 1) / nh)).astype(np.float32).astype(np.float64) * LOG2E
    qfeat = np.zeros((nh, PAIR), np.float32)
    kfeat = np.zeros((seq, PAIR), np.float32)
    pos = np.arange(seq)
    for k, piece in enumerate(_np_bf16_pieces(slopes)):
        qfeat[:, k] = piece
        qfeat[:, N_PIECES + k] = piece
        kfeat[:, k] = pos % MOBA_BLOCK
        kfeat[:, N_PIECES + k] = pos - pos % MOBA_BLOCK
    return jnp.asarray(qfeat), jnp.asarray(kfeat, jnp.bfloat16)


def _fox_placement(nh, d):
    place = np.zeros((N_PIECES * nh, d), np.float32)
    for h in range(nh):
        for k in range(N_PIECES):
            place[k * nh + h, (h // 2) * PAIR + (h % 2) * N_PIECES + k] = 1.0
    return jnp.asarray(place, jnp.bfloat16)


def kernel(x, attn_norm, moba_w_qkv, fox_w_q, w_o, kv_norm, w_kv, w_f, b_f, mlp_norm, w_up, w_down, final_norm):
    batch, seq, d = x.shape
    depth = attn_norm.shape[0]
    n_a = moba_w_qkv.shape[0]
    nh = w_f.shape[1]
    assert d == nh * HEAD_DIM and d % PAIR == 0 and seq % MOBA_BLOCK == 0
    assert seq // MOBA_BLOCK - 1 >= MOBA_TOPK

    bf16 = jnp.bfloat16
    q_scale = LOG2E / math.sqrt(HEAD_DIM)
    moba_qfeat, moba_kfeat = _moba_features(nh, seq)

    stack_rows = lambda v: v.reshape(v.shape[0], 1, v.shape[1])
    row = lambda v: v.reshape(1, -1)
    attn_norms, mlp_norms = stack_rows(attn_norm), stack_rows(mlp_norm)
    w_qkv_b, w_q_b, w_o_b = moba_w_qkv.astype(bf16), fox_w_q.astype(bf16), w_o.astype(bf16)
    w_up_b, w_down_b = w_up.astype(bf16), w_down.astype(bf16)

    h = x.reshape(batch * seq, d)
    k_sh = vt_sh = kf_sh = None
    for layer in range(depth):
        if layer == n_a:
            k_sh, vt_sh, kf_sh = _kv_proj(
                h, row(kv_norm), w_kv.astype(bf16), jnp.tile(w_f, (1, N_PIECES)).astype(bf16),
                row(jnp.tile(b_f, N_PIECES)), _fox_placement(nh, d), batch, seq)
        if layer < n_a:
            qk, vt = _qkv_proj(h, attn_norms, w_qkv_b, layer, batch, seq, q_scale)
            mix = _moba_attention(qk, vt, moba_qfeat, moba_kfeat, batch, seq, d)
        else:
            q = _q_proj(h, attn_norms, w_q_b, layer, layer - n_a, q_scale)
            mix = _fox_attention(q, k_sh, kf_sh, vt_sh, batch, seq, d)
        h = _attn_out_mlp(h, mix, w_o_b, mlp_norms, w_up_b, w_down_b, row(final_norm), layer,
                          final_norm=(layer == depth - 1))
    return h.reshape(batch, seq, d)
```

```python
import functools
import math

import numpy as np
import jax
import jax.numpy as jnp
from jax import lax
from jax.experimental import pallas as pl
from jax.experimental.pallas import tpu as pltpu

RMS_EPS = 1e-6
NEG_INF = -1e30
LOG2E = 1.4426950408889634
MOBA_BLOCK = 256
MOBA_TOPK = 3
HEAD_DIM = 64
PAIR = 2 * HEAD_DIM
V7X_VMEM_BYTES = 64 * 1024 * 1024
ROW_TILE = 1024
KV_ROW_TILE = 512
FF_CHUNK = 1024
N_PIECES = 3
BF16_ROWS = 16
ONES_ROWS = BF16_ROWS
SCORE_SLOTS = 8
PAIRS_PER_STEP = 2
LOOKAHEAD = SCORE_SLOTS - 1

_NT = (((1,), (1,)), ((), ()))


def _vmem_limit(nbytes):
    return int(min(max(nbytes * 3 // 2, 16 * 1024 * 1024), V7X_VMEM_BYTES * 7 // 8))


def _rms(x, g):
    return x * lax.rsqrt(jnp.mean(x * x, axis=-1, keepdims=True) + RMS_EPS) * g


def _const_spec(shape):
    return pl.BlockSpec(shape, lambda *_: (0,) * len(shape), pipeline_mode=pl.Buffered(1))


def _layer_spec(stacked, layer):
    return pl.BlockSpec((None,) + stacked.shape[1:], lambda *_: (layer, 0, 0), pipeline_mode=pl.Buffered(1))


def _first_step():
    return (pl.program_id(0) == 0) & (pl.program_id(1) == 0)


def _transpose_weight(wt_ref, w):
    wt_ref[...] = w.astype(jnp.float32).T.astype(wt_ref.dtype)


def _qkv_kernel(h_ref, g_ref, w_ref, qk_ref, vt_ref, wvt_ref, *, q_scale):
    d = h_ref.shape[1]

    @pl.when(_first_step())
    def _():
        _transpose_weight(wvt_ref, w_ref[:, 2 * d:])

    xn = _rms(h_ref[...], g_ref[...]).astype(jnp.bfloat16)
    q = jnp.dot(xn, w_ref[:, :d], preferred_element_type=jnp.float32)
    qk_ref[:, :d] = (q * q_scale).astype(qk_ref.dtype)
    k = jnp.dot(xn, w_ref[:, d:2 * d], preferred_element_type=jnp.float32)
    qk_ref[:, d:] = k.astype(qk_ref.dtype)
    vt = lax.dot_general(wvt_ref[...], xn, _NT, preferred_element_type=jnp.float32)
    vt_ref[0] = vt.astype(vt_ref.dtype)


def _qkv_proj(h, norms, weights, layer, batch, seq, q_scale):
    n, d = h.shape
    tm = min(ROW_TILE, seq)
    nt = seq // tm
    est = 2 * tm * d * 4 + (weights[0].size + d * d) * 2 + 2 * (tm * 2 * d + d * tm) * 2 + 4 * tm * d * 4 + d * d * 4
    return pl.pallas_call(
        functools.partial(_qkv_kernel, q_scale=q_scale),
        grid=(batch, nt),
        in_specs=[
            pl.BlockSpec((tm, d), lambda b, t: (b * nt + t, 0)),
            _layer_spec(norms, layer),
            _layer_spec(weights, layer),
        ],
        out_specs=[
            pl.BlockSpec((tm, 2 * d), lambda b, t: (b * nt + t, 0)),
            pl.BlockSpec((1, d, tm), lambda b, t: (b, 0, t)),
        ],
        out_shape=[
            jax.ShapeDtypeStruct((n, 2 * d), jnp.bfloat16),
            jax.ShapeDtypeStruct((batch, d, seq), jnp.bfloat16),
        ],
        scratch_shapes=[pltpu.VMEM((d, d), jnp.bfloat16)],
        compiler_params=pltpu.CompilerParams(
            dimension_semantics=("arbitrary", "arbitrary"), vmem_limit_bytes=_vmem_limit(est)),
        name="moba_qkv_proj",
    )(h, norms, weights)


def _q_kernel(h_ref, g_ref, wq_ref, q_ref, *, q_scale):
    xn = _rms(h_ref[...], g_ref[...]).astype(jnp.bfloat16)
    q = jnp.dot(xn, wq_ref[...], preferred_element_type=jnp.float32)
    q_ref[...] = (q * q_scale).astype(q_ref.dtype)


def _q_proj(h, norms, weights, norm_layer, layer, q_scale):
    n, d = h.shape
    tm = min(ROW_TILE, n)
    est = 2 * tm * d * 4 + weights[0].size * 2 + 2 * tm * d * 2 + 3 * tm * d * 4
    return pl.pallas_call(
        functools.partial(_q_kernel, q_scale=q_scale),
        grid=(n // tm,),
        in_specs=[pl.BlockSpec((tm, d), lambda r: (r, 0)), _layer_spec(norms, norm_layer),
                  _layer_spec(weights, layer)],
        out_specs=pl.BlockSpec((tm, d), lambda r: (r, 0)),
        out_shape=jax.ShapeDtypeStruct((n, d), jnp.bfloat16),
        compiler_params=pltpu.CompilerParams(
            dimension_semantics=("parallel",), vmem_limit_bytes=_vmem_limit(est)),
        name="fox_q_proj",
    )(h, norms, weights)


def _bf16_pieces(x):
    p1 = x.astype(jnp.bfloat16)
    r1 = x - p1.astype(jnp.float32)
    p2 = r1.astype(jnp.bfloat16)
    p3 = (r1 - p2.astype(jnp.float32)).astype(jnp.bfloat16)
    return p1, p2, p3


def _kv_kernel(h_ref, g_ref, w_ref, wf_ref, bf_ref, place_ref, k_ref, vt_ref, kf_ref, carry_ref, wvt_ref):
    tm, d = h_ref.shape
    nh = wf_ref.shape[1] // N_PIECES

    @pl.when(_first_step())
    def _():
        _transpose_weight(wvt_ref, w_ref[:, d:])

    @pl.when(pl.program_id(1) == 0)
    def _():
        carry_ref[...] = jnp.zeros_like(carry_ref)

    xn = _rms(h_ref[...], g_ref[...]).astype(jnp.bfloat16)
    k_ref[...] = jnp.dot(xn, w_ref[:, :d], preferred_element_type=jnp.float32).astype(k_ref.dtype)
    vt = lax.dot_general(wvt_ref[...], xn, _NT, preferred_element_type=jnp.float32)
    vt_ref[0] = vt.astype(vt_ref.dtype)

    z = jnp.dot(xn, wf_ref[...], preferred_element_type=jnp.float32) + bf_ref[...]
    log_f = jnp.minimum(z, 0.0) - jnp.log1p(jnp.exp(-jnp.abs(z)))
    row = lax.broadcasted_iota(jnp.int32, (tm, tm), 0)
    col = lax.broadcasted_iota(jnp.int32, (tm, tm), 1)
    tri = jnp.where(col <= row, 1.0, 0.0).astype(jnp.bfloat16)
    c = carry_ref[...]
    for piece in _bf16_pieces(log_f):
        c = c + jnp.dot(tri, piece, preferred_element_type=jnp.float32)
    carry_ref[...] = c[tm - 1:tm, :]
    p1, p2, p3 = _bf16_pieces(c * (-LOG2E))
    col_f = lax.broadcasted_iota(jnp.int32, c.shape, 1)
    pcs = jnp.where(col_f < nh, p1, jnp.where(col_f < 2 * nh, p2, p3))
    kf = jnp.dot(pcs, place_ref[...], preferred_element_type=jnp.float32)
    kf_ref[...] = kf.astype(kf_ref.dtype)


def _kv_proj(h, g, w, wf, bf, place, batch, seq):
    n, d = h.shape
    nf = wf.shape[1]
    tm = min(KV_ROW_TILE, seq)
    nt = seq // tm
    est = 2 * tm * d * 4 + (w.size + d * d) * 2 + 6 * tm * d * 2 + 5 * tm * d * 4 + tm * tm * 8 + d * d * 4
    return pl.pallas_call(
        _kv_kernel,
        grid=(batch, nt),
        in_specs=[
            pl.BlockSpec((tm, d), lambda b, t: (b * nt + t, 0)),
            _const_spec((1, d)),
            _const_spec(w.shape),
            _const_spec(wf.shape),
            _const_spec((1, nf)),
            _const_spec(place.shape),
        ],
        out_specs=[
            pl.BlockSpec((tm, d), lambda b, t: (b * nt + t, 0)),
            pl.BlockSpec((1, d, tm), lambda b, t: (b, 0, t)),
            pl.BlockSpec((tm, d), lambda b, t: (b * nt + t, 0)),
        ],
        out_shape=[
            jax.ShapeDtypeStruct((n, d), jnp.bfloat16),
            jax.ShapeDtypeStruct((batch, d, seq), jnp.bfloat16),
            jax.ShapeDtypeStruct((n, d), jnp.bfloat16),
        ],
        scratch_shapes=[pltpu.VMEM((1, nf), jnp.float32), pltpu.VMEM((d, d), jnp.bfloat16)],
        compiler_params=pltpu.CompilerParams(
            dimension_semantics=("arbitrary", "arbitrary"), vmem_limit_bytes=_vmem_limit(est)),
        name="shared_kv_proj",
    )(h, g, w, wf, bf, place)


def _mlp_kernel(h_ref, mix_ref, wo_ref, g_ref, wup_ref, wdn_ref, gf_ref, out_ref, *, final_norm):
    ff = wup_ref.shape[1]
    fc = min(FF_CHUNK, ff)
    h1 = h_ref[...] + jnp.dot(mix_ref[...], wo_ref[...], preferred_element_type=jnp.float32)
    xn = _rms(h1, g_ref[...]).astype(jnp.bfloat16)
    acc = h1
    for c in range(ff // fc):
        u = jnp.dot(xn, wup_ref[:, c * fc:(c + 1) * fc], preferred_element_type=jnp.float32)
        a = jnp.square(jnp.maximum(u, 0.0)).astype(jnp.bfloat16)
        acc = acc + jnp.dot(a, wdn_ref[c * fc:(c + 1) * fc, :], preferred_element_type=jnp.float32)
    if final_norm:
        acc = _rms(acc, gf_ref[...])
    out_ref[...] = acc


def _attn_out_mlp(h, mix, wo, norms, wup, wdn, gf, layer, final_norm):
    n, d = h.shape
    ff = wup.shape[2]
    tm = min(ROW_TILE, n)
    fc = min(FF_CHUNK, ff)
    est = (4 * tm * d * 4 + 2 * tm * d * 2 + (wo[0].size + wup[0].size + wdn[0].size) * 2
           + 3 * tm * d * 4 + tm * fc * 6)
    return pl.pallas_call(
        functools.partial(_mlp_kernel, final_norm=final_norm),
        grid=(n // tm,),
        in_specs=[
            pl.BlockSpec((tm, d), lambda r: (r, 0)),
            pl.BlockSpec((tm, d), lambda r: (r, 0)),
            _layer_spec(wo, layer),
            _layer_spec(norms, layer),
            _layer_spec(wup, layer),
            _layer_spec(wdn, layer),
            _const_spec((1, d)),
        ],
        out_specs=pl.BlockSpec((tm, d), lambda r: (r, 0)),
        out_shape=jax.ShapeDtypeStruct((n, d), jnp.float32),
        compiler_params=pltpu.CompilerParams(
            dimension_semantics=("parallel",), vmem_limit_bytes=_vmem_limit(est)),
        name="attn_out_mlp",
    )(h, mix, wo, norms, wup, wdn, gf)


def _head_mask(q2, a):
    lane = lax.broadcasted_iota(jnp.int32, q2.shape, 1)
    keep = (lane < HEAD_DIM) if a == 0 else (lane >= HEAD_DIM)
    return jnp.where(keep, q2, jnp.zeros_like(q2))


class _Unit:
    def __init__(self, pp, jj, a, q_ext, shift):
        self.pp, self.jj, self.a, self.q_ext = pp, jj, a, q_ext
        self.nblk = jj + 1
        self.shifts = [shift(i) for i in range(jj)] + [None]
        self.m = None
        self.o = None
        self.slot = None


def _pair_lanes(pp):
    return slice(pp * PAIR, (pp + 1) * PAIR)


def _score_tile(u, i, k_ref, kf_of, x_ref):
    tq = u.q_ext.shape[0]
    rows = slice(i * MOBA_BLOCK, (i + 1) * MOBA_BLOCK)
    k_ext = jnp.concatenate([k_ref[rows, _pair_lanes(u.pp)], kf_of(u.pp, rows)], axis=1)
    x = lax.dot_general(k_ext, u.q_ext, _NT, preferred_element_type=jnp.float32)
    if i == u.nblk - 1:
        key_l = lax.broadcasted_iota(jnp.int32, (MOBA_BLOCK, tq), 0)
        qry_l = lax.broadcasted_iota(jnp.int32, (MOBA_BLOCK, tq), 1)
        x = jnp.where(key_l <= qry_l, x, NEG_INF)
    mi = jnp.max(x, axis=0, keepdims=True)
    if u.shifts[i] is not None:
        mi = mi + u.shifts[i]
    u.m = mi if u.m is None else jnp.maximum(u.m, mi)
    x_ref[u.slot, rows, :] = x


def _value_tile(u, i, vt_ref, x_ref):
    rows = slice(i * MOBA_BLOCK, (i + 1) * MOBA_BLOCK)
    mm = u.m if u.shifts[i] is None else u.m - u.shifts[i]
    pt = jnp.exp2(x_ref[u.slot, rows, :] - mm).astype(jnp.bfloat16)
    ones = jnp.ones((ONES_ROWS, MOBA_BLOCK), jnp.bfloat16)
    v0 = u.pp * PAIR + u.a * HEAD_DIM
    v_ext = jnp.concatenate([vt_ref[0, v0:v0 + HEAD_DIM, rows], ones], axis=0)
    oi = jnp.dot(v_ext, pt, preferred_element_type=jnp.float32)
    u.o = oi if u.o is None else u.o + oi


def _sweep(units, k_ref, kf_of, vt_ref, out_ref, x_ref):
    units = sorted(units, key=lambda u: (u.pp, -u.jj, -u.a))
    for n, u in enumerate(units):
        u.slot = n % SCORE_SLOTS
    for u in units[:LOOKAHEAD]:
        for i in range(u.nblk):
            _score_tile(u, i, k_ref, kf_of, x_ref)
    done = {}
    for n, u in enumerate(units):
        nxt = units[n + LOOKAHEAD] if n + LOOKAHEAD < len(units) else None
        for i in range(max(u.nblk, nxt.nblk if nxt else 0)):
            if nxt is not None and i < nxt.nblk:
                _score_tile(nxt, i, k_ref, kf_of, x_ref)
            if i < u.nblk:
                _value_tile(u, i, vt_ref, x_ref)
        heads = done.setdefault((u.pp, u.jj), {})
        heads[u.a] = u.o[:HEAD_DIM, :] * (1.0 / u.o[HEAD_DIM:HEAD_DIM + 1, :])
        if len(heads) == 2:
            ot = jnp.concatenate([heads[0], heads[1]], axis=0)
            out_ref[u.jj * MOBA_BLOCK:(u.jj + 1) * MOBA_BLOCK, _pair_lanes(u.pp)] = ot.T.astype(out_ref.dtype)
            del done[(u.pp, u.jj)]


def _moba_units(pp, first_head, qf_ref, q_ref, kmean_ref, nkb):
    tq = MOBA_BLOCK
    km_rows = kmean_ref.shape[1]
    km_pieces = jnp.concatenate(_bf16_pieces(kmean_ref[pp]), axis=0)
    for jj in range(nkb):
        q2 = q_ref[jj * MOBA_BLOCK:(jj + 1) * MOBA_BLOCK, _pair_lanes(pp)]
        for a in range(2):
            qa = _head_mask(q2, a)
            qfeat = jnp.broadcast_to(qf_ref[pl.ds(first_head + a, 1), :], (tq, PAIR)).astype(jnp.bfloat16)
            if jj > MOBA_TOPK:
                g = lax.dot_general(km_pieces, qa, _NT, preferred_element_type=jnp.float32)
                gate = sum(g[k * km_rows:k * km_rows + nkb, :] for k in range(N_PIECES))
                blk = lax.broadcasted_iota(jnp.int32, (nkb, tq), 0)
                beaten = jnp.zeros((nkb, tq), jnp.float32)
                for ip in range(jj):
                    row = gate[ip:ip + 1, :]
                    beaten = beaten + jnp.where(row > gate, 1.0, jnp.where((row == gate) & (blk > ip), 1.0, 0.0))
                unsel = jnp.where(beaten < float(MOBA_TOPK), 0.0, NEG_INF)
                shift = lambda i, unsel=unsel: unsel[i:i + 1, :]
            else:
                shift = lambda i: None
            yield _Unit(pp, jj, a, jnp.concatenate([qa, qfeat], axis=1), shift)


def _moba_kernel(qf_ref, kf_ref, q_ref, k_ref, vt_ref, out_ref, kmean_ref, x_ref):
    nkb = q_ref.shape[0] // MOBA_BLOCK
    pps = q_ref.shape[1] // PAIR
    kmean_ref[...] = jnp.zeros_like(kmean_ref)
    units = []
    for pp in range(pps):
        for i in range(nkb):
            kb = k_ref[i * MOBA_BLOCK:(i + 1) * MOBA_BLOCK, _pair_lanes(pp)].astype(jnp.float32)
            kmean_ref[pp, i:i + 1, :] = jnp.sum(kb, axis=0, keepdims=True) * (1.0 / MOBA_BLOCK)
        first_head = 2 * (pl.program_id(1) * pps + pp)
        units += list(_moba_units(pp, first_head, qf_ref, q_ref, kmean_ref, nkb))
    _sweep(units, k_ref, lambda pp, rows: kf_ref[rows, :], vt_ref, out_ref, x_ref)


def _score_scratch(seq):
    return pltpu.VMEM((SCORE_SLOTS, seq, MOBA_BLOCK), jnp.float32)


def _pairs_per_step(npair):
    return PAIRS_PER_STEP if npair % PAIRS_PER_STEP == 0 else 1


def _attn_vmem_estimate(seq, pps):
    blocks = 2 * 5 * seq * pps * PAIR * 2
    temps = SCORE_SLOTS * seq * MOBA_BLOCK * 4 + 8 * MOBA_BLOCK * MOBA_BLOCK * 4
    return blocks + temps


def _moba_attention(qk, vt, qfeat, kfeat, batch, seq, d):
    nq = seq // MOBA_BLOCK
    npair = d // PAIR
    pps = _pairs_per_step(npair)
    ngrp = npair // pps
    est = _attn_vmem_estimate(seq, pps) + qfeat.size * 4
    return pl.pallas_call(
        _moba_kernel,
        grid=(batch, ngrp),
        in_specs=[
            _const_spec(qfeat.shape),
            _const_spec(kfeat.shape),
            pl.BlockSpec((seq, pps * PAIR), lambda b, g: (b, g)),
            pl.BlockSpec((seq, pps * PAIR), lambda b, g: (b, ngrp + g)),
            pl.BlockSpec((1, pps * PAIR, seq), lambda b, g: (b, g, 0)),
        ],
        out_specs=pl.BlockSpec((seq, pps * PAIR), lambda b, g: (b, g)),
        out_shape=jax.ShapeDtypeStruct((batch * seq, d), jnp.bfloat16),
        scratch_shapes=[pltpu.VMEM((pps, -(-nq // BF16_ROWS) * BF16_ROWS, PAIR), jnp.float32), _score_scratch(seq)],
        compiler_params=pltpu.CompilerParams(
            dimension_semantics=("parallel", "parallel"), vmem_limit_bytes=_vmem_limit(est)),
        name="moba_attention",
    )(qfeat, kfeat, qk, qk, vt)


def _fox_units(pp, q_ref):
    for jj in range(q_ref.shape[0] // MOBA_BLOCK):
        q2 = q_ref[jj * MOBA_BLOCK:(jj + 1) * MOBA_BLOCK, _pair_lanes(pp)]
        lane = lax.broadcasted_iota(jnp.int32, q2.shape, 1)
        for a in range(2):
            qfeat = jnp.where((lane >= a * N_PIECES) & (lane < (a + 1) * N_PIECES), 1.0, 0.0).astype(jnp.bfloat16)
            yield _Unit(pp, jj, a, jnp.concatenate([_head_mask(q2, a), qfeat], axis=1), lambda i: None)


def _fox_kernel(q_ref, k_ref, kf_ref, vt_ref, out_ref, x_ref):
    units = [u for pp in range(q_ref.shape[1] // PAIR) for u in _fox_units(pp, q_ref)]
    _sweep(units, k_ref, lambda pp, rows: kf_ref[rows, _pair_lanes(pp)], vt_ref, out_ref, x_ref)


def _fox_attention(q, k, kfeat, vt, batch, seq, d):
    npair = d // PAIR
    pps = _pairs_per_step(npair)
    est = _attn_vmem_estimate(seq, pps)
    return pl.pallas_call(
        _fox_kernel,
        grid=(batch, npair // pps),
        in_specs=[
            pl.BlockSpec((seq, pps * PAIR), lambda b, g: (b, g)),
            pl.BlockSpec((seq, pps * PAIR), lambda b, g: (b, g)),
            pl.BlockSpec((seq, pps * PAIR), lambda b, g: (b, g)),
            pl.BlockSpec((1, pps * PAIR, seq), lambda b, g: (b, g, 0)),
        ],
        out_specs=pl.BlockSpec((seq, pps * PAIR), lambda b, g: (b, g)),
        out_shape=jax.ShapeDtypeStruct((batch * seq, d), jnp.bfloat16),
        scratch_shapes=[_score_scratch(seq)],
        compiler_params=pltpu.CompilerParams(
            dimension_semantics=("parallel", "parallel"), vmem_limit_bytes=_vmem_limit(est)),
        name="fox_attention",
    )(q, k, kfeat, vt)


def _np_bf16_pieces(x):
    out = []
    r = np.asarray(x, np.float64)
    for _ in range(N_PIECES):
        p = r.astype(np.float32).astype(jnp.bfloat16).astype(np.float64)
        out.append(p.astype(np.float32))
        r = r - p
    return out


def _moba_features(nh, seq):
    slopes = (2.0 ** (-8.0 * np.arange(1, nh + 1) / nh)).astype(np.float32).astype(np.float64) * LOG2E
    qfeat = np.zeros((nh, PAIR), np.float32)
    kfeat = np.zeros((seq, PAIR), np.float32)
    pos = np.arange(seq)
    for k, piece in enumerate(_np_bf16_pieces(slopes)):
        qfeat[:, k] = piece
        qfeat[:, N_PIECES + k] = piece
        kfeat[:, k] = pos % MOBA_BLOCK
        kfeat[:, N_PIECES + k] = pos - pos % MOBA_BLOCK
    return jnp.asarray(qfeat), jnp.asarray(kfeat, jnp.bfloat16)


def _fox_placement(nh, d):
    place = np.zeros((N_PIECES * nh, d), np.float32)
    for h in range(nh):
        for k in range(N_PIECES):
            place[k * nh + h, (h // 2) * PAIR + (h % 2) * N_PIECES + k] = 1.0
    return jnp.asarray(place, jnp.bfloat16)


def kernel(x, attn_norm, moba_w_qkv, fox_w_q, w_o, kv_norm, w_kv, w_f, b_f, mlp_norm, w_up, w_down, final_norm):
    batch, seq, d = x.shape
    depth = attn_norm.shape[0]
    n_a = moba_w_qkv.shape[0]
    nh = w_f.shape[1]
    assert d == nh * HEAD_DIM and d % PAIR == 0 and seq % MOBA_BLOCK == 0
    assert seq // MOBA_BLOCK - 1 >= MOBA_TOPK

    bf16 = jnp.bfloat16
    q_scale = LOG2E / math.sqrt(HEAD_DIM)
    moba_qfeat, moba_kfeat = _moba_features(nh, seq)

    stack_rows = lambda v: v.reshape(v.shape[0], 1, v.shape[1])
    row = lambda v: v.reshape(1, -1)
    attn_norms, mlp_norms = stack_rows(attn_norm), stack_rows(mlp_norm)
    w_qkv_b, w_q_b, w_o_b = moba_w_qkv.astype(bf16), fox_w_q.astype(bf16), w_o.astype(bf16)
    w_up_b, w_down_b = w_up.astype(bf16), w_down.astype(bf16)

    h = x.reshape(batch * seq, d)
    k_sh = vt_sh = kf_sh = None
    for layer in range(depth):
        if layer == n_a:
            k_sh, vt_sh, kf_sh = _kv_proj(
                h, row(kv_norm), w_kv.astype(bf16), jnp.tile(w_f, (1, N_PIECES)).astype(bf16),
                row(jnp.tile(b_f, N_PIECES)), _fox_placement(nh, d), batch, seq)
        if layer < n_a:
            qk, vt = _qkv_proj(h, attn_norms, w_qkv_b, layer, batch, seq, q_scale)
            mix = _moba_attention(qk, vt, moba_qfeat, moba_kfeat, batch, seq, d)
        else:
            q = _q_proj(h, attn_norms, w_q_b, layer, layer - n_a, q_scale)
            mix = _fox_attention(q, k_sh, kf_sh, vt_sh, batch, seq, d)
        h = _attn_out_mlp(h, mix, w_o_b, mlp_norms, w_up_b, w_down_b, row(final_norm), layer,
                          final_norm=(layer == depth - 1))
    return h.reshape(batch, seq, d)
```

```python
import functools
import math

import numpy as np
import jax
import jax.numpy as jnp
from jax import lax
from jax.experimental import pallas as pl
from jax.experimental.pallas import tpu as pltpu

RMS_EPS = 1e-6
NEG_INF = -1e30
LOG2E = 1.4426950408889634
MOBA_BLOCK = 256
MOBA_TOPK = 3
HEAD_DIM = 64
PAIR = 2 * HEAD_DIM
V7X_VMEM_BYTES = 64 * 1024 * 1024
ROW_TILE = 1024
KV_ROW_TILE = 512
FF_CHUNK = 1024
N_PIECES = 3
BF16_ROWS = 16
ONES_ROWS = BF16_ROWS
SCORE_SLOTS = 8
PAIRS_PER_STEP = 2
LOOKAHEAD = SCORE_SLOTS - 1

_NT = (((1,), (1,)), ((), ()))


def _vmem_limit(nbytes):
    return int(min(max(nbytes * 3 // 2, 16 * 1024 * 1024), V7X_VMEM_BYTES * 7 // 8))


def _rms(x, g):
    return x * lax.rsqrt(jnp.mean(x * x, axis=-1, keepdims=True) + RMS_EPS) * g


def _const_spec(shape):
    return pl.BlockSpec(shape, lambda *_: (0,) * len(shape), pipeline_mode=pl.Buffered(1))


def _layer_spec(stacked, layer):
    return pl.BlockSpec((None,) + stacked.shape[1:], lambda *_: (layer, 0, 0), pipeline_mode=pl.Buffered(1))


def _first_step():
    return (pl.program_id(0) == 0) & (pl.program_id(1) == 0)


def _transpose_weight(wt_ref, w):
    wt_ref[...] = w.astype(jnp.float32).T.astype(wt_ref.dtype)


def _qkv_kernel(h_ref, g_ref, w_ref, qk_ref, vt_ref, wvt_ref, *, q_scale):
    d = h_ref.shape[1]

    @pl.when(_first_step())
    def _():
        _transpose_weight(wvt_ref, w_ref[:, 2 * d:])

    xn = _rms(h_ref[...], g_ref[...]).astype(jnp.bfloat16)
    q = jnp.dot(xn, w_ref[:, :d], preferred_element_type=jnp.float32)
    qk_ref[:, :d] = (q * q_scale).astype(qk_ref.dtype)
    k = jnp.dot(xn, w_ref[:, d:2 * d], preferred_element_type=jnp.float32)
    qk_ref[:, d:] = k.astype(qk_ref.dtype)
    vt = lax.dot_general(wvt_ref[...], xn, _NT, preferred_element_type=jnp.float32)
    vt_ref[0] = vt.astype(vt_ref.dtype)


def _qkv_proj(h, norms, weights, layer, batch, seq, q_scale):
    n, d = h.shape
    tm = min(ROW_TILE, seq)
    nt = seq // tm
    est = 2 * tm * d * 4 + (weights[0].size + d * d) * 2 + 2 * (tm * 2 * d + d * tm) * 2 + 4 * tm * d * 4 + d * d * 4
    return pl.pallas_call(
        functools.partial(_qkv_kernel, q_scale=q_scale),
        grid=(batch, nt),
        in_specs=[
            pl.BlockSpec((tm, d), lambda b, t: (b * nt + t, 0)),
            _layer_spec(norms, layer),
            _layer_spec(weights, layer),
        ],
        out_specs=[
            pl.BlockSpec((tm, 2 * d), lambda b, t: (b * nt + t, 0)),
            pl.BlockSpec((1, d, tm), lambda b, t: (b, 0, t)),
        ],
        out_shape=[
            jax.ShapeDtypeStruct((n, 2 * d), jnp.bfloat16),
            jax.ShapeDtypeStruct((batch, d, seq), jnp.bfloat16),
        ],
        scratch_shapes=[pltpu.VMEM((d, d), jnp.bfloat16)],
        compiler_params=pltpu.CompilerParams(
            dimension_semantics=("arbitrary", "arbitrary"), vmem_limit_bytes=_vmem_limit(est)),
        name="moba_qkv_proj",
    )(h, norms, weights)


def _q_kernel(h_ref, g_ref, wq_ref, q_ref, *, q_scale):
    xn = _rms(h_ref[...], g_ref[...]).astype(jnp.bfloat16)
    q = jnp.dot(xn, wq_ref[...], preferred_element_type=jnp.float32)
    q_ref[...] = (q * q_scale).astype(q_ref.dtype)


def _q_proj(h, norms, weights, norm_layer, layer, q_scale):
    n, d = h.shape
    tm = min(ROW_TILE, n)
    est = 2 * tm * d * 4 + weights[0].size * 2 + 2 * tm * d * 2 + 3 * tm * d * 4
    return pl.pallas_call(
        functools.partial(_q_kernel, q_scale=q_scale),
        grid=(n // tm,),
        in_specs=[pl.BlockSpec((tm, d), lambda r: (r, 0)), _layer_spec(norms, norm_layer),
                  _layer_spec(weights, layer)],
        out_specs=pl.BlockSpec((tm, d), lambda r: (r, 0)),
        out_shape=jax.ShapeDtypeStruct((n, d), jnp.bfloat16),
        compiler_params=pltpu.CompilerParams(
            dimension_semantics=("parallel",), vmem_limit_bytes=_vmem_limit(est)),
        name="fox_q_proj",
    )(h, norms, weights)


def _bf16_pieces(x):
    p1 = x.astype(jnp.bfloat16)
    r1 = x - p1.astype(jnp.float32)
    p2 = r1.astype(jnp.bfloat16)
    p3 = (r1 - p2.astype(jnp.float32)).astype(jnp.bfloat16)
    return p1, p2, p3


def _kv_kernel(h_ref, g_ref, w_ref, wf_ref, bf_ref, place_ref, k_ref, vt_ref, kf_ref, carry_ref, wvt_ref):
    tm, d = h_ref.shape
    nh = wf_ref.shape[1] // N_PIECES

    @pl.when(_first_step())
    def _():
        _transpose_weight(wvt_ref, w_ref[:, d:])

    @pl.when(pl.program_id(1) == 0)
    def _():
        carry_ref[...] = jnp.zeros_like(carry_ref)

    xn = _rms(h_ref[...], g_ref[...]).astype(jnp.bfloat16)
    z = jnp.dot(xn, wf_ref[...], preferred_element_type=jnp.float32) + bf_ref[...]
    k_ref[...] = jnp.dot(xn, w_ref[:, :d], preferred_element_type=jnp.float32).astype(k_ref.dtype)

    log_f = jnp.minimum(z, 0.0) - jnp.log1p(jnp.exp(-jnp.abs(z)))
    row = lax.broadcasted_iota(jnp.int32, (tm, tm), 0)
    col = lax.broadcasted_iota(jnp.int32, (tm, tm), 1)
    tri = jnp.where(col <= row, 1.0, 0.0).astype(jnp.bfloat16)
    c = carry_ref[...]
    for piece in _bf16_pieces(log_f):
        c = c + jnp.dot(tri, piece, preferred_element_type=jnp.float32)
    carry_ref[...] = c[tm - 1:tm, :]

    vt = lax.dot_general(wvt_ref[...], xn, _NT, preferred_element_type=jnp.float32)
    vt_ref[0] = vt.astype(vt_ref.dtype)

    p1, p2, p3 = _bf16_pieces(c * (-LOG2E))
    col_f = lax.broadcasted_iota(jnp.int32, c.shape, 1)
    pcs = jnp.where(col_f < nh, p1, jnp.where(col_f < 2 * nh, p2, p3))
    kf = jnp.dot(pcs, place_ref[...], preferred_element_type=jnp.float32)
    kf_ref[...] = kf.astype(kf_ref.dtype)


def _kv_proj(h, g, w, wf, bf, place, batch, seq):
    n, d = h.shape
    nf = wf.shape[1]
    tm = min(KV_ROW_TILE, seq)
    nt = seq // tm
    est = 2 * tm * d * 4 + (w.size + d * d) * 2 + 6 * tm * d * 2 + 5 * tm * d * 4 + tm * tm * 8 + d * d * 4
    return pl.pallas_call(
        _kv_kernel,
        grid=(batch, nt),
        in_specs=[
            pl.BlockSpec((tm, d), lambda b, t: (b * nt + t, 0)),
            _const_spec((1, d)),
            _const_spec(w.shape),
            _const_spec(wf.shape),
            _const_spec((1, nf)),
            _const_spec(place.shape),
        ],
        out_specs=[
            pl.BlockSpec((tm, d), lambda b, t: (b * nt + t, 0)),
            pl.BlockSpec((1, d, tm), lambda b, t: (b, 0, t)),
            pl.BlockSpec((tm, d), lambda b, t: (b * nt + t, 0)),
        ],
        out_shape=[
            jax.ShapeDtypeStruct((n, d), jnp.bfloat16),
            jax.ShapeDtypeStruct((batch, d, seq), jnp.bfloat16),
            jax.ShapeDtypeStruct((n, d), jnp.bfloat16),
        ],
        scratch_shapes=[pltpu.VMEM((1, nf), jnp.float32), pltpu.VMEM((d, d), jnp.bfloat16)],
        compiler_params=pltpu.CompilerParams(
            dimension_semantics=("arbitrary", "arbitrary"), vmem_limit_bytes=_vmem_limit(est)),
        name="shared_kv_proj",
    )(h, g, w, wf, bf, place)


def _mlp_kernel(h_ref, mix_ref, wo_ref, g_ref, wup_ref, wdn_ref, gf_ref, out_ref, *, final_norm):
    ff = wup_ref.shape[1]
    fc = min(FF_CHUNK, ff)
    h1 = h_ref[...] + jnp.dot(mix_ref[...], wo_ref[...], preferred_element_type=jnp.float32)
    xn = _rms(h1, g_ref[...]).astype(jnp.bfloat16)
    acc = h1
    for c in range(ff // fc):
        u = jnp.dot(xn, wup_ref[:, c * fc:(c + 1) * fc], preferred_element_type=jnp.float32)
        a = jnp.square(jnp.maximum(u, 0.0)).astype(jnp.bfloat16)
        acc = acc + jnp.dot(a, wdn_ref[c * fc:(c + 1) * fc, :], preferred_element_type=jnp.float32)
    if final_norm:
        acc = _rms(acc, gf_ref[...])
    out_ref[...] = acc


def _attn_out_mlp(h, mix, wo, norms, wup, wdn, gf, layer, final_norm):
    n, d = h.shape
    ff = wup.shape[2]
    tm = min(ROW_TILE, n)
    fc = min(FF_CHUNK, ff)
    est = (4 * tm * d * 4 + 2 * tm * d * 2 + (wo[0].size + wup[0].size + wdn[0].size) * 2
           + 3 * tm * d * 4 + tm * fc * 6)
    return pl.pallas_call(
        functools.partial(_mlp_kernel, final_norm=final_norm),
        grid=(n // tm,),
        in_specs=[
            pl.BlockSpec((tm, d), lambda r: (r, 0)),
            pl.BlockSpec((tm, d), lambda r: (r, 0)),
            _layer_spec(wo, layer),
            _layer_spec(norms, layer),
            _layer_spec(wup, layer),
            _layer_spec(wdn, layer),
            _const_spec((1, d)),
        ],
        out_specs=pl.BlockSpec((tm, d), lambda r: (r, 0)),
        out_shape=jax.ShapeDtypeStruct((n, d), jnp.float32),
        compiler_params=pltpu.CompilerParams(
            dimension_semantics=("parallel",), vmem_limit_bytes=_vmem_limit(est)),
        name="attn_out_mlp",
    )(h, mix, wo, norms, wup, wdn, gf)


def _head_mask(q2, a):
    lane = lax.broadcasted_iota(jnp.int32, q2.shape, 1)
    keep = (lane < HEAD_DIM) if a == 0 else (lane >= HEAD_DIM)
    return jnp.where(keep, q2, jnp.zeros_like(q2))


class _Unit:
    def __init__(self, pp, jj, a, q_ext, shift):
        self.pp, self.jj, self.a, self.q_ext = pp, jj, a, q_ext
        self.nblk = jj + 1
        self.shifts = [shift(i) for i in range(jj)] + [None]
        self.m = None
        self.o = None
        self.slot = None


def _pair_lanes(pp):
    return slice(pp * PAIR, (pp + 1) * PAIR)


def _score_tile(u, i, k_ref, kf_of, x_ref):
    tq = u.q_ext.shape[0]
    rows = slice(i * MOBA_BLOCK, (i + 1) * MOBA_BLOCK)
    k_ext = jnp.concatenate([k_ref[rows, _pair_lanes(u.pp)], kf_of(u.pp, rows)], axis=1)
    x = lax.dot_general(k_ext, u.q_ext, _NT, preferred_element_type=jnp.float32)
    if i == u.nblk - 1:
        key_l = lax.broadcasted_iota(jnp.int32, (MOBA_BLOCK, tq), 0)
        qry_l = lax.broadcasted_iota(jnp.int32, (MOBA_BLOCK, tq), 1)
        x = jnp.where(key_l <= qry_l, x, NEG_INF)
    x_ref[u.slot, rows, :] = x


def _value_tile(u, i, vt_ref, x_ref):
    rows = slice(i * MOBA_BLOCK, (i + 1) * MOBA_BLOCK)
    mi = jnp.max(x_ref[u.slot, rows, :], axis=0, keepdims=True)
    if u.shifts[i] is not None:
        mi = mi + u.shifts[i]
    m_new = mi if u.m is None else jnp.maximum(u.m, mi)
    mm = m_new if u.shifts[i] is None else m_new - u.shifts[i]
    pt = jnp.exp2(x_ref[u.slot, rows, :] - mm).astype(jnp.bfloat16)
    ones = jnp.ones((ONES_ROWS, MOBA_BLOCK), jnp.bfloat16)
    v0 = u.pp * PAIR + u.a * HEAD_DIM
    v_ext = jnp.concatenate([vt_ref[0, v0:v0 + HEAD_DIM, rows], ones], axis=0)
    oi = jnp.dot(v_ext, pt, preferred_element_type=jnp.float32)
    u.o = oi if u.o is None else jnp.exp2(u.m - m_new) * u.o + oi
    u.m = m_new


def _sweep(units, k_ref, kf_of, vt_ref, out_ref, x_ref):
    units = sorted(units, key=lambda u: (u.pp, -u.jj, -u.a))
    for n, u in enumerate(units):
        u.slot = n % SCORE_SLOTS
    for u in units[:LOOKAHEAD]:
        for i in reversed(range(u.nblk)):
            _score_tile(u, i, k_ref, kf_of, x_ref)
    done = {}
    for n, u in enumerate(units):
        nxt = units[n + LOOKAHEAD] if n + LOOKAHEAD < len(units) else None
        for t in range(max(u.nblk, nxt.nblk if nxt else 0)):
            if nxt is not None and t < nxt.nblk:
                _score_tile(nxt, nxt.nblk - 1 - t, k_ref, kf_of, x_ref)
            if t < u.nblk:
                _value_tile(u, u.nblk - 1 - t, vt_ref, x_ref)
        heads = done.setdefault((u.pp, u.jj), {})
        heads[u.a] = u.o[:HEAD_DIM, :] * (1.0 / u.o[HEAD_DIM:HEAD_DIM + 1, :])
        if len(heads) == 2:
            ot = jnp.concatenate([heads[0], heads[1]], axis=0)
            out_ref[u.jj * MOBA_BLOCK:(u.jj + 1) * MOBA_BLOCK, _pair_lanes(u.pp)] = ot.T.astype(out_ref.dtype)
            del done[(u.pp, u.jj)]


def _moba_units(pp, first_head, qf_ref, q_ref, kmean_ref, nkb):
    tq = MOBA_BLOCK
    km_rows = kmean_ref.shape[1]
    km_pieces = jnp.concatenate(_bf16_pieces(kmean_ref[pp]), axis=0)
    for jj in range(nkb):
        q2 = q_ref[jj * MOBA_BLOCK:(jj + 1) * MOBA_BLOCK, _pair_lanes(pp)]
        for a in range(2):
            qa = _head_mask(q2, a)
            qfeat = jnp.broadcast_to(qf_ref[pl.ds(first_head + a, 1), :], (tq, PAIR)).astype(jnp.bfloat16)
            if jj > MOBA_TOPK:
                g = lax.dot_general(km_pieces, qa, _NT, preferred_element_type=jnp.float32)
                gate = sum(g[k * km_rows:k * km_rows + nkb, :] for k in range(N_PIECES))
                blk = lax.broadcasted_iota(jnp.int32, (nkb, tq), 0)
                beaten = jnp.zeros((nkb, tq), jnp.float32)
                for ip in range(jj):
                    row = gate[ip:ip + 1, :]
                    beaten = beaten + jnp.where(row > gate, 1.0, jnp.where((row == gate) & (blk > ip), 1.0, 0.0))
                unsel = jnp.where(beaten < float(MOBA_TOPK), 0.0, NEG_INF)
                shift = lambda i, unsel=unsel: unsel[i:i + 1, :]
            else:
                shift = lambda i: None
            yield _Unit(pp, jj, a, jnp.concatenate([qa, qfeat], axis=1), shift)


def _moba_kernel(qf_ref, kf_ref, q_ref, k_ref, vt_ref, out_ref, kmean_ref, x_ref):
    nkb = q_ref.shape[0] // MOBA_BLOCK
    pps = q_ref.shape[1] // PAIR
    kmean_ref[...] = jnp.zeros_like(kmean_ref)
    units = []
    for pp in range(pps):
        for i in range(nkb):
            kb = k_ref[i * MOBA_BLOCK:(i + 1) * MOBA_BLOCK, _pair_lanes(pp)].astype(jnp.float32)
            kmean_ref[pp, i:i + 1, :] = jnp.sum(kb, axis=0, keepdims=True) * (1.0 / MOBA_BLOCK)
        first_head = 2 * (pl.program_id(1) * pps + pp)
        units += list(_moba_units(pp, first_head, qf_ref, q_ref, kmean_ref, nkb))
    _sweep(units, k_ref, lambda pp, rows: kf_ref[rows, :], vt_ref, out_ref, x_ref)


def _score_scratch(seq):
    return pltpu.VMEM((SCORE_SLOTS, seq, MOBA_BLOCK), jnp.float32)


def _pairs_per_step(npair):
    return PAIRS_PER_STEP if npair % PAIRS_PER_STEP == 0 else 1


def _attn_vmem_estimate(seq, pps):
    blocks = 2 * 5 * seq * pps * PAIR * 2
    temps = SCORE_SLOTS * seq * MOBA_BLOCK * 4 + 8 * MOBA_BLOCK * MOBA_BLOCK * 4
    return blocks + temps


def _moba_attention(qk, vt, qfeat, kfeat, batch, seq, d):
    nq = seq // MOBA_BLOCK
    npair = d // PAIR
    pps = _pairs_per_step(npair)
    ngrp = npair // pps
    est = _attn_vmem_estimate(seq, pps) + qfeat.size * 4
    return pl.pallas_call(
        _moba_kernel,
        grid=(batch, ngrp),
        in_specs=[
            _const_spec(qfeat.shape),
            _const_spec(kfeat.shape),
            pl.BlockSpec((seq, pps * PAIR), lambda b, g: (b, g)),
            pl.BlockSpec((seq, pps * PAIR), lambda b, g: (b, ngrp + g)),
            pl.BlockSpec((1, pps * PAIR, seq), lambda b, g: (b, g, 0)),
        ],
        out_specs=pl.BlockSpec((seq, pps * PAIR), lambda b, g: (b, g)),
        out_shape=jax.ShapeDtypeStruct((batch * seq, d), jnp.bfloat16),
        scratch_shapes=[pltpu.VMEM((pps, -(-nq // BF16_ROWS) * BF16_ROWS, PAIR), jnp.float32), _score_scratch(seq)],
        compiler_params=pltpu.CompilerParams(
            dimension_semantics=("parallel", "parallel"), vmem_limit_bytes=_vmem_limit(est)),
        name="moba_attention",
    )(qfeat, kfeat, qk, qk, vt)


def _fox_units(pp, q_ref):
    for jj in range(q_ref.shape[0] // MOBA_BLOCK):
        q2 = q_ref[jj * MOBA_BLOCK:(jj + 1) * MOBA_BLOCK, _pair_lanes(pp)]
        lane = lax.broadcasted_iota(jnp.int32, q2.shape, 1)
        for a in range(2):
            qfeat = jnp.where((lane >= a * N_PIECES) & (lane < (a + 1) * N_PIECES), 1.0, 0.0).astype(jnp.bfloat16)
            yield _Unit(pp, jj, a, jnp.concatenate([_head_mask(q2, a), qfeat], axis=1), lambda i: None)


def _fox_kernel(q_ref, k_ref, kf_ref, vt_ref, out_ref, x_ref):
    units = [u for pp in range(q_ref.shape[1] // PAIR) for u in _fox_units(pp, q_ref)]
    _sweep(units, k_ref, lambda pp, rows: kf_ref[rows, _pair_lanes(pp)], vt_ref, out_ref, x_ref)


def _fox_attention(q, k, kfeat, vt, batch, seq, d):
    npair = d // PAIR
    pps = _pairs_per_step(npair)
    est = _attn_vmem_estimate(seq, pps)
    return pl.pallas_call(
        _fox_kernel,
        grid=(batch, npair // pps),
        in_specs=[
            pl.BlockSpec((seq, pps * PAIR), lambda b, g: (b, g)),
            pl.BlockSpec((seq, pps * PAIR), lambda b, g: (b, g)),
            pl.BlockSpec((seq, pps * PAIR), lambda b, g: (b, g)),
            pl.BlockSpec((1, pps * PAIR, seq), lambda b, g: (b, g, 0)),
        ],
        out_specs=pl.BlockSpec((seq, pps * PAIR), lambda b, g: (b, g)),
        out_shape=jax.ShapeDtypeStruct((batch * seq, d), jnp.bfloat16),
        scratch_shapes=[_score_scratch(seq)],
        compiler_params=pltpu.CompilerParams(
            dimension_semantics=("parallel", "parallel"), vmem_limit_bytes=_vmem_limit(est)),
        name="fox_attention",
    )(q, k, kfeat, vt)


def _np_bf16_pieces(x):
    out = []
    r = np.asarray(x, np.float64)
    for _ in range(N_PIECES):
        p = r.astype(np.float32).astype(jnp.bfloat16).astype(np.float64)
        out.append(p.astype(np.float32))
        r = r - p
    return out


def _moba_features(nh, seq):
    slopes = (2.0 ** (-8.0 * np.arange(1, nh + 1) / nh)).astype(np.float32).astype(np.float64) * LOG2E
    qfeat = np.zeros((nh, PAIR), np.float32)
    kfeat = np.zeros((seq, PAIR), np.float32)
    pos = np.arange(seq)
    for k, piece in enumerate(_np_bf16_pieces(slopes)):
        qfeat[:, k] = piece
        qfeat[:, N_PIECES + k] = piece
        kfeat[:, k] = pos % MOBA_BLOCK
        kfeat[:, N_PIECES + k] = pos - pos % MOBA_BLOCK
    return jnp.asarray(qfeat), jnp.asarray(kfeat, jnp.bfloat16)


def _fox_placement(nh, d):
    place = np.zeros((N_PIECES * nh, d), np.float32)
    for h in range(nh):
        for k in range(N_PIECES):
            place[k * nh + h, (h // 2) * PAIR + (h % 2) * N_PIECES + k] = 1.0
    return jnp.asarray(place, jnp.bfloat16)


def kernel(x, attn_norm, moba_w_qkv, fox_w_q, w_o, kv_norm, w_kv, w_f, b_f, mlp_norm, w_up, w_down, final_norm):
    batch, seq, d = x.shape
    depth = attn_norm.shape[0]
    n_a = moba_w_qkv.shape[0]
    nh = w_f.shape[1]
    assert d == nh * HEAD_DIM and d % PAIR == 0 and seq % MOBA_BLOCK == 0
    assert seq // MOBA_BLOCK - 1 >= MOBA_TOPK

    bf16 = jnp.bfloat16
    q_scale = LOG2E / math.sqrt(HEAD_DIM)
    moba_qfeat, moba_kfeat = _moba_features(nh, seq)

    stack_rows = lambda v: v.reshape(v.shape[0], 1, v.shape[1])
    row = lambda v: v.reshape(1, -1)
    attn_norms, mlp_norms = stack_rows(attn_norm), stack_rows(mlp_norm)
    w_qkv_b, w_q_b, w_o_b = moba_w_qkv.astype(bf16), fox_w_q.astype(bf16), w_o.astype(bf16)
    w_up_b, w_down_b = w_up.astype(bf16), w_down.astype(bf16)

    h = x.reshape(batch * seq, d)
    k_sh = vt_sh = kf_sh = None
    for layer in range(depth):
        if layer == n_a:
            k_sh, vt_sh, kf_sh = _kv_proj(
                h, row(kv_norm), w_kv.astype(bf16), jnp.tile(w_f, (1, N_PIECES)).astype(bf16),
                row(jnp.tile(b_f, N_PIECES)), _fox_placement(nh, d), batch, seq)
        if layer < n_a:
            qk, vt = _qkv_proj(h, attn_norms, w_qkv_b, layer, batch, seq, q_scale)
            mix = _moba_attention(qk, vt, moba_qfeat, moba_kfeat, batch, seq, d)
        else:
            q = _q_proj(h, attn_norms, w_q_b, layer, layer - n_a, q_scale)
            mix = _fox_attention(q, k_sh, kf_sh, vt_sh, batch, seq, d)
        h = _attn_out_mlp(h, mix, w_o_b, mlp_norms, w_up_b, w_down_b, row(final_norm), layer,
                          final_norm=(layer == depth - 1))
    return h.reshape(batch, seq, d)
```

```python
import functools
import math

import numpy as np
import jax
import jax.numpy as jnp
from jax import lax
from jax.experimental import pallas as pl
from jax.experimental.pallas import tpu as pltpu

RMS_EPS = 1e-6
NEG_INF = -1e30
LOG2E = 1.4426950408889634
MOBA_BLOCK = 256
MOBA_TOPK = 3
HEAD_DIM = 64
PAIR = 2 * HEAD_DIM
V7X_VMEM_BYTES = 64 * 1024 * 1024
ROW_TILE = 1024
KV_ROW_TILE = 512
FF_CHUNK = 1024
N_PIECES = 3
BF16_ROWS = 16
ONES_ROWS = BF16_ROWS
SCORE_SLOTS = 5
PAIRS_PER_STEP = 2
LOOKAHEAD = SCORE_SLOTS - 1

_NT = (((1,), (1,)), ((), ()))


def _vmem_limit(nbytes):
    return int(min(max(nbytes * 3 // 2, 16 * 1024 * 1024), V7X_VMEM_BYTES * 7 // 8))


def _rms(x, g):
    return x * lax.rsqrt(jnp.mean(x * x, axis=-1, keepdims=True) + RMS_EPS) * g


def _const_spec(shape):
    return pl.BlockSpec(shape, lambda *_: (0,) * len(shape), pipeline_mode=pl.Buffered(1))


def _layer_spec(stacked, layer):
    return pl.BlockSpec((None,) + stacked.shape[1:], lambda *_: (layer, 0, 0), pipeline_mode=pl.Buffered(1))


def _first_step():
    return (pl.program_id(0) == 0) & (pl.program_id(1) == 0)


def _transpose_weight(wt_ref, w):
    wt_ref[...] = w.astype(jnp.float32).T.astype(wt_ref.dtype)


def _qkv_kernel(h_ref, g_ref, w_ref, qk_ref, vt_ref, wvt_ref, *, q_scale):
    d = h_ref.shape[1]

    @pl.when(_first_step())
    def _():
        _transpose_weight(wvt_ref, w_ref[:, 2 * d:])

    xn = _rms(h_ref[...], g_ref[...]).astype(jnp.bfloat16)
    q = jnp.dot(xn, w_ref[:, :d], preferred_element_type=jnp.float32)
    qk_ref[:, :d] = (q * q_scale).astype(qk_ref.dtype)
    k = jnp.dot(xn, w_ref[:, d:2 * d], preferred_element_type=jnp.float32)
    qk_ref[:, d:] = k.astype(qk_ref.dtype)
    vt = lax.dot_general(wvt_ref[...], xn, _NT, preferred_element_type=jnp.float32)
    vt_ref[0] = vt.astype(vt_ref.dtype)


def _qkv_proj(h, norms, weights, layer, batch, seq, q_scale):
    n, d = h.shape
    tm = min(ROW_TILE, seq)
    nt = seq // tm
    est = 2 * tm * d * 4 + (weights[0].size + d * d) * 2 + 2 * (tm * 2 * d + d * tm) * 2 + 4 * tm * d * 4 + d * d * 4
    return pl.pallas_call(
        functools.partial(_qkv_kernel, q_scale=q_scale),
        grid=(batch, nt),
        in_specs=[
            pl.BlockSpec((tm, d), lambda b, t: (b * nt + t, 0)),
            _layer_spec(norms, layer),
            _layer_spec(weights, layer),
        ],
        out_specs=[
            pl.BlockSpec((tm, 2 * d), lambda b, t: (b * nt + t, 0)),
            pl.BlockSpec((1, d, tm), lambda b, t: (b, 0, t)),
        ],
        out_shape=[
            jax.ShapeDtypeStruct((n, 2 * d), jnp.bfloat16),
            jax.ShapeDtypeStruct((batch, d, seq), jnp.bfloat16),
        ],
        scratch_shapes=[pltpu.VMEM((d, d), jnp.bfloat16)],
        compiler_params=pltpu.CompilerParams(
            dimension_semantics=("arbitrary", "arbitrary"), vmem_limit_bytes=_vmem_limit(est)),
        name="moba_qkv_proj",
    )(h, norms, weights)


def _q_kernel(h_ref, g_ref, wq_ref, q_ref, *, q_scale):
    xn = _rms(h_ref[...], g_ref[...]).astype(jnp.bfloat16)
    q = jnp.dot(xn, wq_ref[...], preferred_element_type=jnp.float32)
    q_ref[...] = (q * q_scale).astype(q_ref.dtype)


def _q_proj(h, norms, weights, norm_layer, layer, q_scale):
    n, d = h.shape
    tm = min(ROW_TILE, n)
    est = 2 * tm * d * 4 + weights[0].size * 2 + 2 * tm * d * 2 + 3 * tm * d * 4
    return pl.pallas_call(
        functools.partial(_q_kernel, q_scale=q_scale),
        grid=(n // tm,),
        in_specs=[pl.BlockSpec((tm, d), lambda r: (r, 0)), _layer_spec(norms, norm_layer),
                  _layer_spec(weights, layer)],
        out_specs=pl.BlockSpec((tm, d), lambda r: (r, 0)),
        out_shape=jax.ShapeDtypeStruct((n, d), jnp.bfloat16),
        compiler_params=pltpu.CompilerParams(
            dimension_semantics=("parallel",), vmem_limit_bytes=_vmem_limit(est)),
        name="fox_q_proj",
    )(h, norms, weights)


def _bf16_pieces(x):
    p1 = x.astype(jnp.bfloat16)
    r1 = x - p1.astype(jnp.float32)
    p2 = r1.astype(jnp.bfloat16)
    p3 = (r1 - p2.astype(jnp.float32)).astype(jnp.bfloat16)
    return p1, p2, p3


def _kv_kernel(h_ref, g_ref, w_ref, wf_ref, bf_ref, place_ref, k_ref, vt_ref, kf_ref, carry_ref, wvt_ref):
    tm, d = h_ref.shape
    nh = wf_ref.shape[1] // N_PIECES

    @pl.when(_first_step())
    def _():
        _transpose_weight(wvt_ref, w_ref[:, d:])

    @pl.when(pl.program_id(1) == 0)
    def _():
        carry_ref[...] = jnp.zeros_like(carry_ref)

    xn = _rms(h_ref[...], g_ref[...]).astype(jnp.bfloat16)
    z = jnp.dot(xn, wf_ref[...], preferred_element_type=jnp.float32) + bf_ref[...]
    k_ref[...] = jnp.dot(xn, w_ref[:, :d], preferred_element_type=jnp.float32).astype(k_ref.dtype)

    log_f = jnp.minimum(z, 0.0) - jnp.log1p(jnp.exp(-jnp.abs(z)))
    row = lax.broadcasted_iota(jnp.int32, (tm, tm), 0)
    col = lax.broadcasted_iota(jnp.int32, (tm, tm), 1)
    tri = jnp.where(col <= row, 1.0, 0.0).astype(jnp.bfloat16)
    c = carry_ref[...]
    for piece in _bf16_pieces(log_f):
        c = c + jnp.dot(tri, piece, preferred_element_type=jnp.float32)
    carry_ref[...] = c[tm - 1:tm, :]

    vt = lax.dot_general(wvt_ref[...], xn, _NT, preferred_element_type=jnp.float32)
    vt_ref[0] = vt.astype(vt_ref.dtype)

    p1, p2, p3 = _bf16_pieces(c * (-LOG2E))
    col_f = lax.broadcasted_iota(jnp.int32, c.shape, 1)
    pcs = jnp.where(col_f < nh, p1, jnp.where(col_f < 2 * nh, p2, p3))
    kf = jnp.dot(pcs, place_ref[...], preferred_element_type=jnp.float32)
    kf_ref[...] = kf.astype(kf_ref.dtype)


def _kv_proj(h, g, w, wf, bf, place, batch, seq):
    n, d = h.shape
    nf = wf.shape[1]
    tm = min(KV_ROW_TILE, seq)
    nt = seq // tm
    est = 2 * tm * d * 4 + (w.size + d * d) * 2 + 6 * tm * d * 2 + 5 * tm * d * 4 + tm * tm * 8 + d * d * 4
    return pl.pallas_call(
        _kv_kernel,
        grid=(batch, nt),
        in_specs=[
            pl.BlockSpec((tm, d), lambda b, t: (b * nt + t, 0)),
            _const_spec((1, d)),
            _const_spec(w.shape),
            _const_spec(wf.shape),
            _const_spec((1, nf)),
            _const_spec(place.shape),
        ],
        out_specs=[
            pl.BlockSpec((tm, d), lambda b, t: (b * nt + t, 0)),
            pl.BlockSpec((1, d, tm), lambda b, t: (b, 0, t)),
            pl.BlockSpec((tm, d), lambda b, t: (b * nt + t, 0)),
        ],
        out_shape=[
            jax.ShapeDtypeStruct((n, d), jnp.bfloat16),
            jax.ShapeDtypeStruct((batch, d, seq), jnp.bfloat16),
            jax.ShapeDtypeStruct((n, d), jnp.bfloat16),
        ],
        scratch_shapes=[pltpu.VMEM((1, nf), jnp.float32), pltpu.VMEM((d, d), jnp.bfloat16)],
        compiler_params=pltpu.CompilerParams(
            dimension_semantics=("arbitrary", "arbitrary"), vmem_limit_bytes=_vmem_limit(est)),
        name="shared_kv_proj",
    )(h, g, w, wf, bf, place)


def _mlp_kernel(h_ref, mix_ref, wo_ref, g_ref, wup_ref, wdn_ref, gf_ref, out_ref, *, final_norm):
    ff = wup_ref.shape[1]
    fc = min(FF_CHUNK, ff)
    h1 = h_ref[...] + jnp.dot(mix_ref[...], wo_ref[...], preferred_element_type=jnp.float32)
    xn = _rms(h1, g_ref[...]).astype(jnp.bfloat16)
    acc = h1
    for c in range(ff // fc):
        u = jnp.dot(xn, wup_ref[:, c * fc:(c + 1) * fc], preferred_element_type=jnp.float32)
        a = jnp.square(jnp.maximum(u, 0.0)).astype(jnp.bfloat16)
        acc = acc + jnp.dot(a, wdn_ref[c * fc:(c + 1) * fc, :], preferred_element_type=jnp.float32)
    if final_norm:
        acc = _rms(acc, gf_ref[...])
    out_ref[...] = acc


def _attn_out_mlp(h, mix, wo, norms, wup, wdn, gf, layer, final_norm):
    n, d = h.shape
    ff = wup.shape[2]
    tm = min(ROW_TILE, n)
    fc = min(FF_CHUNK, ff)
    est = (4 * tm * d * 4 + 2 * tm * d * 2 + (wo[0].size + wup[0].size + wdn[0].size) * 2
           + 3 * tm * d * 4 + tm * fc * 6)
    return pl.pallas_call(
        functools.partial(_mlp_kernel, final_norm=final_norm),
        grid=(n // tm,),
        in_specs=[
            pl.BlockSpec((tm, d), lambda r: (r, 0)),
            pl.BlockSpec((tm, d), lambda r: (r, 0)),
            _layer_spec(wo, layer),
            _layer_spec(norms, layer),
            _layer_spec(wup, layer),
            _layer_spec(wdn, layer),
            _const_spec((1, d)),
        ],
        out_specs=pl.BlockSpec((tm, d), lambda r: (r, 0)),
        out_shape=jax.ShapeDtypeStruct((n, d), jnp.float32),
        compiler_params=pltpu.CompilerParams(
            dimension_semantics=("parallel",), vmem_limit_bytes=_vmem_limit(est)),
        name="attn_out_mlp",
    )(h, mix, wo, norms, wup, wdn, gf)


def _head_mask(q2, a):
    lane = lax.broadcasted_iota(jnp.int32, q2.shape, 1)
    keep = (lane < HEAD_DIM) if a == 0 else (lane >= HEAD_DIM)
    return jnp.where(keep, q2, jnp.zeros_like(q2))


class _Unit:
    def __init__(self, pp, jj, a, q_ext, shift):
        self.pp, self.jj, self.a, self.q_ext = pp, jj, a, q_ext
        self.nblk = jj + 1
        self.shifts = [shift(i) for i in range(jj)] + [None]
        self.m = None
        self.o = None
        self.slot = None


def _pair_lanes(pp):
    return slice(pp * PAIR, (pp + 1) * PAIR)


def _score_tile(u, i, k_ref, kf_of, x_ref):
    tq = u.q_ext.shape[0]
    rows = slice(i * MOBA_BLOCK, (i + 1) * MOBA_BLOCK)
    k_ext = jnp.concatenate([k_ref[rows, _pair_lanes(u.pp)], kf_of(u.pp, rows)], axis=1)
    x = lax.dot_general(k_ext, u.q_ext, _NT, preferred_element_type=jnp.float32)
    if i == u.nblk - 1:
        key_l = lax.broadcasted_iota(jnp.int32, (MOBA_BLOCK, tq), 0)
        qry_l = lax.broadcasted_iota(jnp.int32, (MOBA_BLOCK, tq), 1)
        x = jnp.where(key_l <= qry_l, x, NEG_INF)
    x_ref[u.slot, rows, :] = x


def _value_tile(u, i, vt_ref, x_ref):
    rows = slice(i * MOBA_BLOCK, (i + 1) * MOBA_BLOCK)
    mi = jnp.max(x_ref[u.slot, rows, :], axis=0, keepdims=True)
    if u.shifts[i] is not None:
        mi = mi + u.shifts[i]
    m_new = mi if u.m is None else jnp.maximum(u.m, mi)
    mm = m_new if u.shifts[i] is None else m_new - u.shifts[i]
    pt = jnp.exp2(x_ref[u.slot, rows, :] - mm).astype(jnp.bfloat16)
    ones = jnp.ones((ONES_ROWS, MOBA_BLOCK), jnp.bfloat16)
    v0 = u.pp * PAIR + u.a * HEAD_DIM
    v_ext = jnp.concatenate([vt_ref[0, v0:v0 + HEAD_DIM, rows], ones], axis=0)
    oi = jnp.dot(v_ext, pt, preferred_element_type=jnp.float32)
    u.o = oi if u.o is None else jnp.exp2(u.m - m_new) * u.o + oi
    u.m = m_new


def _sweep(units, k_ref, kf_of, vt_ref, out_ref, x_ref):
    units = sorted(units, key=lambda u: (u.pp, -u.jj, -u.a))
    for n, u in enumerate(units):
        u.slot = n % SCORE_SLOTS
    for u in units[:LOOKAHEAD]:
        for i in reversed(range(u.nblk)):
            _score_tile(u, i, k_ref, kf_of, x_ref)
    done = {}
    for n, u in enumerate(units):
        nxt = units[n + LOOKAHEAD] if n + LOOKAHEAD < len(units) else None
        for t in range(max(u.nblk, nxt.nblk if nxt else 0)):
            if nxt is not None and t < nxt.nblk:
                _score_tile(nxt, nxt.nblk - 1 - t, k_ref, kf_of, x_ref)
            if t < u.nblk:
                _value_tile(u, u.nblk - 1 - t, vt_ref, x_ref)
        heads = done.setdefault((u.pp, u.jj), {})
        heads[u.a] = u.o[:HEAD_DIM, :] * (1.0 / u.o[HEAD_DIM:HEAD_DIM + 1, :])
        if len(heads) == 2:
            ot = jnp.concatenate([heads[0], heads[1]], axis=0)
            out_ref[u.jj * MOBA_BLOCK:(u.jj + 1) * MOBA_BLOCK, _pair_lanes(u.pp)] = ot.T.astype(out_ref.dtype)
            del done[(u.pp, u.jj)]


def _moba_units(pp, first_head, qf_ref, q_ref, kmean_ref, nkb):
    tq = MOBA_BLOCK
    km_rows = kmean_ref.shape[1]
    km_pieces = jnp.concatenate(_bf16_pieces(kmean_ref[pp]), axis=0)
    for jj in range(nkb):
        q2 = q_ref[jj * MOBA_BLOCK:(jj + 1) * MOBA_BLOCK, _pair_lanes(pp)]
        for a in range(2):
            qa = _head_mask(q2, a)
            qfeat = jnp.broadcast_to(qf_ref[pl.ds(first_head + a, 1), :], (tq, PAIR)).astype(jnp.bfloat16)
            if jj > MOBA_TOPK:
                g = lax.dot_general(km_pieces, qa, _NT, preferred_element_type=jnp.float32)
                gate = sum(g[k * km_rows:k * km_rows + nkb, :] for k in range(N_PIECES))
                blk = lax.broadcasted_iota(jnp.int32, (nkb, tq), 0)
                beaten = jnp.zeros((nkb, tq), jnp.float32)
                for ip in range(jj):
                    row = gate[ip:ip + 1, :]
                    beaten = beaten + jnp.where(row > gate, 1.0, jnp.where((row == gate) & (blk > ip), 1.0, 0.0))
                unsel = jnp.where(beaten < float(MOBA_TOPK), 0.0, NEG_INF)
                shift = lambda i, unsel=unsel: unsel[i:i + 1, :]
            else:
                shift = lambda i: None
            yield _Unit(pp, jj, a, jnp.concatenate([qa, qfeat], axis=1), shift)


def _moba_kernel(qf_ref, kf_ref, q_ref, k_ref, vt_ref, out_ref, kmean_ref, x_ref):
    nkb = q_ref.shape[0] // MOBA_BLOCK
    pps = q_ref.shape[1] // PAIR
    kmean_ref[...] = jnp.zeros_like(kmean_ref)
    units = []
    for pp in range(pps):
        for i in range(nkb):
            kb = k_ref[i * MOBA_BLOCK:(i + 1) * MOBA_BLOCK, _pair_lanes(pp)].astype(jnp.float32)
            kmean_ref[pp, i:i + 1, :] = jnp.sum(kb, axis=0, keepdims=True) * (1.0 / MOBA_BLOCK)
        first_head = 2 * (pl.program_id(1) * pps + pp)
        units += list(_moba_units(pp, first_head, qf_ref, q_ref, kmean_ref, nkb))
    _sweep(units, k_ref, lambda pp, rows: kf_ref[rows, :], vt_ref, out_ref, x_ref)


def _score_scratch(seq):
    return pltpu.VMEM((SCORE_SLOTS, seq, MOBA_BLOCK), jnp.float32)


def _pairs_per_step(npair):
    return PAIRS_PER_STEP if npair % PAIRS_PER_STEP == 0 else 1


def _attn_vmem_estimate(seq, pps):
    blocks = 2 * 5 * seq * pps * PAIR * 2
    temps = SCORE_SLOTS * seq * MOBA_BLOCK * 4 + 8 * MOBA_BLOCK * MOBA_BLOCK * 4
    return blocks + temps


def _moba_attention(qk, vt, qfeat, kfeat, batch, seq, d):
    nq = seq // MOBA_BLOCK
    npair = d // PAIR
    pps = _pairs_per_step(npair)
    ngrp = npair // pps
    est = _attn_vmem_estimate(seq, pps) + qfeat.size * 4
    return pl.pallas_call(
        _moba_kernel,
        grid=(batch, ngrp),
        in_specs=[
            _const_spec(qfeat.shape),
            _const_spec(kfeat.shape),
            pl.BlockSpec((seq, pps * PAIR), lambda b, g: (b, g)),
            pl.BlockSpec((seq, pps * PAIR), lambda b, g: (b, ngrp + g)),
            pl.BlockSpec((1, pps * PAIR, seq), lambda b, g: (b, g, 0)),
        ],
        out_specs=pl.BlockSpec((seq, pps * PAIR), lambda b, g: (b, g)),
        out_shape=jax.ShapeDtypeStruct((batch * seq, d), jnp.bfloat16),
        scratch_shapes=[pltpu.VMEM((pps, -(-nq // BF16_ROWS) * BF16_ROWS, PAIR), jnp.float32), _score_scratch(seq)],
        compiler_params=pltpu.CompilerParams(
            dimension_semantics=("parallel", "parallel"), vmem_limit_bytes=_vmem_limit(est)),
        name="moba_attention",
    )(qfeat, kfeat, qk, qk, vt)


def _fox_units(pp, q_ref):
    for jj in range(q_ref.shape[0] // MOBA_BLOCK):
        q2 = q_ref[jj * MOBA_BLOCK:(jj + 1) * MOBA_BLOCK, _pair_lanes(pp)]
        lane = lax.broadcasted_iota(jnp.int32, q2.shape, 1)
        for a in range(2):
            qfeat = jnp.where((lane >= a * N_PIECES) & (lane < (a + 1) * N_PIECES), 1.0, 0.0).astype(jnp.bfloat16)
            yield _Unit(pp, jj, a, jnp.concatenate([_head_mask(q2, a), qfeat], axis=1), lambda i: None)


def _fox_kernel(q_ref, k_ref, kf_ref, vt_ref, out_ref, x_ref):
    units = [u for pp in range(q_ref.shape[1] // PAIR) for u in _fox_units(pp, q_ref)]
    _sweep(units, k_ref, lambda pp, rows: kf_ref[rows, _pair_lanes(pp)], vt_ref, out_ref, x_ref)


def _fox_attention(q, k, kfeat, vt, batch, seq, d):
    npair = d // PAIR
    pps = _pairs_per_step(npair)
    est = _attn_vmem_estimate(seq, pps)
    return pl.pallas_call(
        _fox_kernel,
        grid=(batch, npair // pps),
        in_specs=[
            pl.BlockSpec((seq, pps * PAIR), lambda b, g: (b, g)),
            pl.BlockSpec((seq, pps * PAIR), lambda b, g: (b, g)),
            pl.BlockSpec((seq, pps * PAIR), lambda b, g: (b, g)),
            pl.BlockSpec((1, pps * PAIR, seq), lambda b, g: (b, g, 0)),
        ],
        out_specs=pl.BlockSpec((seq, pps * PAIR), lambda b, g: (b, g)),
        out_shape=jax.ShapeDtypeStruct((batch * seq, d), jnp.bfloat16),
        scratch_shapes=[_score_scratch(seq)],
        compiler_params=pltpu.CompilerParams(
            dimension_semantics=("parallel", "parallel"), vmem_limit_bytes=_vmem_limit(est)),
        name="fox_attention",
    )(q, k, kfeat, vt)


def _np_bf16_pieces(x):
    out = []
    r = np.asarray(x, np.float64)
    for _ in range(N_PIECES):
        p = r.astype(np.float32).astype(jnp.bfloat16).astype(np.float64)
        out.append(p.astype(np.float32))
        r = r - p
    return out


def _moba_features(nh, seq):
    slopes = (2.0 ** (-8.0 * np.arange(1, nh + 1) / nh)).astype(np.float32).astype(np.float64) * LOG2E
    qfeat = np.zeros((nh, PAIR), np.float32)
    kfeat = np.zeros((seq, PAIR), np.float32)
    pos = np.arange(seq)
    for k, piece in enumerate(_np_bf16_pieces(slopes)):
        qfeat[:, k] = piece
        qfeat[:, N_PIECES + k] = piece
        kfeat[:, k] = pos % MOBA_BLOCK
        kfeat[:, N_PIECES + k] = pos - pos % MOBA_BLOCK
    return jnp.asarray(qfeat), jnp.asarray(kfeat, jnp.bfloat16)


def _fox_placement(nh, d):
    place = np.zeros((N_PIECES * nh, d), np.float32)
    for h in range(nh):
        for k in range(N_PIECES):
            place[k * nh + h, (h // 2) * PAIR + (h % 2) * N_PIECES + k] = 1.0
    return jnp.asarray(place, jnp.bfloat16)


def kernel(x, attn_norm, moba_w_qkv, fox_w_q, w_o, kv_norm, w_kv, w_f, b_f, mlp_norm, w_up, w_down, final_norm):
    batch, seq, d = x.shape
    depth = attn_norm.shape[0]
    n_a = moba_w_qkv.shape[0]
    nh = w_f.shape[1]
    assert d == nh * HEAD_DIM and d % PAIR == 0 and seq % MOBA_BLOCK == 0
    assert seq // MOBA_BLOCK - 1 >= MOBA_TOPK

    bf16 = jnp.bfloat16
    q_scale = LOG2E / math.sqrt(HEAD_DIM)
    moba_qfeat, moba_kfeat = _moba_features(nh, seq)

    stack_rows = lambda v: v.reshape(v.shape[0], 1, v.shape[1])
    row = lambda v: v.reshape(1, -1)
    attn_norms, mlp_norms = stack_rows(attn_norm), stack_rows(mlp_norm)
    w_qkv_b, w_q_b, w_o_b = moba_w_qkv.astype(bf16), fox_w_q.astype(bf16), w_o.astype(bf16)
    w_up_b, w_down_b = w_up.astype(bf16), w_down.astype(bf16)

    h = x.reshape(batch * seq, d)
    k_sh = vt_sh = kf_sh = None
    for layer in range(depth):
        if layer == n_a:
            k_sh, vt_sh, kf_sh = _kv_proj(
                h, row(kv_norm), w_kv.astype(bf16), jnp.tile(w_f, (1, N_PIECES)).astype(bf16),
                row(jnp.tile(b_f, N_PIECES)), _fox_placement(nh, d), batch, seq)
        if layer < n_a:
            qk, vt = _qkv_proj(h, attn_norms, w_qkv_b, layer, batch, seq, q_scale)
            mix = _moba_attention(qk, vt, moba_qfeat, moba_kfeat, batch, seq, d)
        else:
            q = _q_proj(h, attn_norms, w_q_b, layer, layer - n_a, q_scale)
            mix = _fox_attention(q, k_sh, kf_sh, vt_sh, batch, seq, d)
        h = _attn_out_mlp(h, mix, w_o_b, mlp_norms, w_up_b, w_down_b, row(final_norm), layer,
                          final_norm=(layer == depth - 1))
    return h.reshape(batch, seq, d)
```
